```python
import jax, jax.numpy as jnp
from jax import lax
import numpy as np

D_MODEL = 4096
BATCH = 16
SEQ = 256
DEPTH = 2
DEC_BATCH = 8
DEC_SEQ = 1024
PAST_LEN = 512

GRID_W = 64
GLA_HEADS = 8
GLA_DK = 64
GLA_DV = 128
GLA_RANK = 16
GLA_NORMALIZER = 16.0
RET_HEADS = 8
RET_DK = 64
RET_DV = 128
GQA_HEADS = 8
GQA_KV_HEADS = 2
GQA_DIM = 128
NAT_HEADS = 8
NAT_DIM = 128
NAT_ROWS = 8
NAT_COLS = 16
CHUNK = 64
Q_BLOCK = 128
ROPE_THETA = 10000.0
LN_EPS = 1e-6
MIX_WIDTH = GLA_HEADS * GLA_DV + RET_HEADS * RET_DV + GQA_HEADS * GQA_DIM + NAT_HEADS * NAT_DIM
N_EXPERTS = 32
TOP_K = 4
EXPERT_FF = 2048
SWIGLU_LIMIT = 7.0
SWIGLU_ALPHA = 1.702
MOE_BLOCK = 128
DEEPNORM_ALPHA = (2.0 * DEPTH) ** 0.25
DEEPNORM_BETA = (8.0 * DEPTH) ** -0.25
PROJ_SIZES = (GLA_HEADS * GLA_DK, GLA_HEADS * GLA_DK, GLA_HEADS * GLA_DV, GLA_HEADS * GLA_DV, 2 * GLA_RANK,
              RET_HEADS * RET_DK, RET_HEADS * RET_DK, RET_HEADS * RET_DV, RET_HEADS * RET_DV,
              GQA_HEADS * GQA_DIM, GQA_KV_HEADS * GQA_DIM, GQA_KV_HEADS * GQA_DIM,
              NAT_HEADS * NAT_DIM, NAT_HEADS * NAT_DIM, NAT_HEADS * NAT_DIM)
PROJ_WIDTH = sum(PROJ_SIZES)
PROJ_OFFSETS = tuple(int(v) for v in np.cumsum(PROJ_SIZES)[:-1])

kernel_name = 'hybrid_diffusion_parallel_heads_step'


def layer_norm(x):
    xf = x.astype(jnp.float32)
    mu = jnp.mean(xf, -1, keepdims=True)
    var = jnp.mean(jnp.square(xf - mu), -1, keepdims=True)
    return ((xf - mu) * lax.rsqrt(var + LN_EPS)).astype(x.dtype)


def affine_layer_norm(x, g, b):
    return layer_norm(x) * g + b


def rms_norm(x, g):
    xf = x.astype(jnp.float32)
    return (xf * lax.rsqrt(jnp.mean(jnp.square(xf), -1, keepdims=True) + LN_EPS)).astype(x.dtype) * g


def axial_rope(x):
    T, d = x.shape[1], x.shape[-1]
    half = d // 2
    quarter = half // 2
    t = jnp.arange(T)
    freqs = ROPE_THETA ** (-jnp.arange(quarter, dtype=jnp.float32) / quarter)

    def rot(xp, pos):
        ang = pos.astype(jnp.float32)[:, None] * freqs[None, :]
        cos = jnp.cos(ang)[None, :, None, :]
        sin = jnp.sin(ang)[None, :, None, :]
        x1 = xp[..., :quarter].astype(jnp.float32)
        x2 = xp[..., quarter:].astype(jnp.float32)
        return jnp.concatenate([x1 * cos - x2 * sin, x2 * cos + x1 * sin], -1)

    out = jnp.concatenate([rot(x[..., :half], t // GRID_W), rot(x[..., half:], t % GRID_W)], -1)
    return out.astype(x.dtype)


def decay_linear_attn(q, k, v, log_a, s0):
    B, T, H, dk = q.shape
    dv = v.shape[-1]
    n = T // CHUNK
    f32 = jnp.float32
    qc = q.astype(f32).reshape(B, n, CHUNK, H, dk)
    kc = k.astype(f32).reshape(B, n, CHUNK, H, dk)
    vc = v.astype(f32).reshape(B, n, CHUNK, H, dv)
    b = jnp.cumsum(log_a.astype(f32).reshape(B, n, CHUNK, H, dk), axis=2)
    b_last = b[:, :, -1:]
    q_in = qc * jnp.exp(b)
    k_in = kc * jnp.exp(-b)
    k_out = kc * jnp.exp(b_last - b)
    causal = jnp.tril(jnp.ones((CHUNK, CHUNK), bool))
    scores = jnp.where(causal, jnp.einsum('bnihd,bnjhd->bnhij', q_in, k_in), 0.0)
    o_intra = jnp.einsum('bnhij,bnjhv->bnihv', scores, vc)
    kv = jnp.einsum('bnjhd,bnjhv->nbhdv', k_out, vc)
    decay = jnp.exp(b_last[:, :, 0]).transpose(1, 0, 2, 3)

    def step(s, inp):
        dec, kv_c = inp
        return dec[..., None] * s + kv_c, s

    s_final, s_prev = lax.scan(step, s0.astype(f32), (decay, kv))
    o_inter = jnp.einsum('bnihd,nbhdv->bnihv', q_in, s_prev)
    return (o_intra + o_inter).reshape(B, T, H, dv), s_final


def bidir_decay_attn(q, k, v, log_a_f, log_a_b, s0):
    o_f, s_f = decay_linear_attn(q, k, v, log_a_f, s0[:, 0])
    flip = lambda t: jnp.flip(t, axis=1)
    o_b, s_b = decay_linear_attn(flip(q), flip(k), flip(v), flip(log_a_b), s0[:, 1])
    return o_f + flip(o_b), jnp.stack([s_f, s_b], axis=1)


def retention_log_decays():
    expo = -5.0 - jnp.arange(RET_HEADS, dtype=jnp.float32)
    lg = jnp.log1p(-jnp.exp2(expo))
    return lg, lg[::-1]


def blocked_attention(q, k, v):
    B, Tq, H, d = q.shape
    hkv = k.shape[2]
    g = H // hkv
    nb = Tq // Q_BLOCK
    scale = d ** -0.5
    qb = q.reshape(B, nb, Q_BLOCK, hkv, g, d).transpose(1, 0, 2, 3, 4, 5)

    def one_block(qblk):
        s = jnp.einsum('bqkgd,bskd->bkgqs', qblk, k).astype(jnp.float32) * scale
        p = jax.nn.softmax(s, axis=-1).astype(v.dtype)
        return jnp.einsum('bkgqs,bskd->bqkgd', p, v)

    o = lax.map(one_block, qb)
    return o.transpose(1, 0, 2, 3, 4, 5).reshape(B, Tq, H * d)


def neighbourhood_attention(q, k, v, k_ctx, v_ctx, rpb):
    B, T, H, d = q.shape
    rows = T // GRID_W
    kr = min(NAT_ROWS, rows)
    scale = d ** -0.5
    qg = q.reshape(B, rows, GRID_W, H, d)
    kg = k.reshape(B, rows, GRID_W, H, d)
    vg = v.reshape(B, rows, GRID_W, H, d)
    r = jnp.arange(rows)
    row_idx = jnp.clip(r - kr // 2, 0, rows - kr)[:, None] + jnp.arange(kr)[None, :]
    k_band = kg[:, row_idx].reshape(B, rows, kr * GRID_W, H, d)
    v_band = vg[:, row_idx].reshape(B, rows, kr * GRID_W, H, d)
    c = jnp.arange(GRID_W)
    col_start = jnp.clip(c - NAT_COLS // 2, 0, GRID_W - NAT_COLS)
    col_ok = (c[None, :] >= col_start[:, None]) & (c[None, :] < col_start[:, None] + NAT_COLS)
    win = jnp.broadcast_to(col_ok[:, None, :], (GRID_W, kr, GRID_W)).reshape(GRID_W, kr * GRID_W)
    rel_r = row_idx - r[:, None] + NAT_ROWS - 1
    rel_c = jnp.clip(c[None, :] - c[:, None], 1 - NAT_COLS, NAT_COLS - 1) + NAT_COLS - 1
    bias = rpb[:, rel_r[:, None, :, None], rel_c[None, :, None, :]]
    bias = bias.reshape(H, rows, GRID_W, kr * GRID_W).transpose(1, 0, 2, 3).astype(jnp.float32)
    s_loc = jnp.einsum('brqhd,brkhd->brhqk', qg, k_band).astype(jnp.float32) * scale + bias
    s_loc = jnp.where(win, s_loc, -jnp.inf)
    s_ctx = jnp.einsum('brqhd,bkhd->brhqk', qg, k_ctx).astype(jnp.float32) * scale
    p = jax.nn.softmax(jnp.concatenate([s_loc, s_ctx], -1), axis=-1).astype(v.dtype)
    n_loc = kr * GRID_W
    o = (jnp.einsum('brhqk,brkhd->brqhd', p[..., :n_loc], v_band)
         + jnp.einsum('brhqk,bkhd->brqhd', p[..., n_loc:], v_ctx))
    return o.reshape(B, T, H * d)


def token_mix(h, lp, ctx):
    B, T, _ = h.shape
    latent = ctx is not None
    f32 = jnp.float32
    proj = jnp.einsum('btd,de->bte', h, lp['w_in'])
    (aq, ak, av, ag, alr, bq, bk, bv, bg, cq, ck, cv, dq, dk, dv) = jnp.split(proj, PROJ_OFFSETS, axis=-1)

    qa = aq.reshape(B, T, GLA_HEADS, GLA_DK) * GLA_DK ** -0.5
    ka = ak.reshape(B, T, GLA_HEADS, GLA_DK)
    va = av.reshape(B, T, GLA_HEADS, GLA_DV)
    z = jnp.einsum('btnr,nre->btne', alr.reshape(B, T, 2, GLA_RANK), lp['gla_lr_up']) + lp['gla_lr_b']
    log_a = (jax.nn.log_sigmoid(z.astype(f32)) / GLA_NORMALIZER).reshape(B, T, 2, GLA_HEADS, GLA_DK)
    s0a = ctx[0] if latent else jnp.zeros((B, 2, GLA_HEADS, GLA_DK, GLA_DV), f32)
    o_a, st_a = bidir_decay_attn(qa, ka, va, log_a[:, :, 0], log_a[:, :, 1], s0a)
    y_a = rms_norm(o_a, lp['gla_norm_g']).reshape(B, T, -1).astype(h.dtype) * jax.nn.silu(ag)

    qb = bq.reshape(B, T, RET_HEADS, RET_DK)
    kb = bk.reshape(B, T, RET_HEADS, RET_DK) * RET_DK ** -0.5
    vb = bv.reshape(B, T, RET_HEADS, RET_DV)
    if latent:
        qb = axial_rope(qb)
        kb = axial_rope(kb)
    lg_f, lg_b = retention_log_decays()
    dshape = (B, T, RET_HEADS, RET_DK)
    s0b = ctx[1] if latent else jnp.zeros((B, 2, RET_HEADS, RET_DK, RET_DV), f32)
    o_b, st_b = bidir_decay_attn(qb, kb, vb, jnp.broadcast_to(lg_f[:, None], dshape),
                                 jnp.broadcast_to(lg_b[:, None], dshape), s0b)
    y_b = (layer_norm(o_b).reshape(B, T, -1) * lp['ret_gn_g'] + lp['ret_gn_b']).astype(h.dtype) * jax.nn.silu(bg)

    qc = rms_norm(cq.reshape(B, T, GQA_HEADS, GQA_DIM), lp['gqa_qn_g'])
    kc = rms_norm(ck.reshape(B, T, GQA_KV_HEADS, GQA_DIM), lp['gqa_kn_g'])
    vc = cv.reshape(B, T, GQA_KV_HEADS, GQA_DIM)
    if latent:
        y_c = blocked_attention(axial_rope(qc), jnp.concatenate([axial_rope(kc), ctx[2].astype(kc.dtype)], 1),
                                jnp.concatenate([vc, ctx[3].astype(vc.dtype)], 1))
    else:
        y_c = blocked_attention(qc, kc, vc)

    qd = dq.reshape(B, T, NAT_HEADS, NAT_DIM)
    kd = dk.reshape(B, T, NAT_HEADS, NAT_DIM)
    vd = dv.reshape(B, T, NAT_HEADS, NAT_DIM)
    if latent:
        y_d = neighbourhood_attention(qd, kd, vd, ctx[4].astype(kd.dtype), ctx[5].astype(vd.dtype), lp['nat_rpb'])
    else:
        y_d = blocked_attention(qd, kd, vd)

    out = jnp.einsum('bte,ed->btd', jnp.concatenate([y_a, y_b, y_c, y_d], -1), lp['w_o'])
    if latent:
        return out, None
    return out, (st_a.astype(h.dtype), st_b.astype(h.dtype), kc, vc, kd, vd)


def moe_ffn(h, lp):
    B, T, D = h.shape
    x = h.reshape(-1, D)
    n = x.shape[0]
    nk = n * TOP_K
    logits = jnp.einsum('nd,de->ne', x, lp['router_w']).astype(jnp.float32) + lp['router_b']
    top_val, top_idx = lax.top_k(logits, TOP_K)
    gates = jax.nn.softmax(top_val, axis=-1)
    flat_e = top_idx.reshape(-1)
    flat_tok = jnp.repeat(jnp.arange(n, dtype=jnp.int32), TOP_K)
    flat_g = gates.reshape(-1)
    order = jnp.argsort(flat_e)
    se = flat_e[order]
    counts = jnp.bincount(flat_e, length=N_EXPERTS)
    padded = (counts + MOE_BLOCK - 1) // MOE_BLOCK * MOE_BLOCK
    pad_end = jnp.cumsum(padded)
    pad_start = pad_end - padded
    start = jnp.cumsum(counts) - counts
    dest = pad_start[se] + jnp.arange(nk, dtype=jnp.int32) - start[se]
    n_blocks = (nk + N_EXPERTS * (MOE_BLOCK - 1)) // MOE_BLOCK
    n_slots = n_blocks * MOE_BLOCK
    slot_tok = jnp.full((n_slots,), n, jnp.int32).at[dest].set(flat_tok[order])
    slot_gate = jnp.zeros((n_slots,), jnp.float32).at[dest].set(flat_g[order])
    block_expert = jnp.minimum(jnp.searchsorted(pad_end, jnp.arange(n_blocks) * MOE_BLOCK, side='right'),
                               N_EXPERTS - 1)
    x_pad = jnp.concatenate([x, jnp.zeros((1, D), x.dtype)], 0)
    w_gu, b_gu, w_down, b_down = lp['w_gu'], lp['b_gu'], lp['w_down'], lp['b_down']

    def expert_block(args):
        tok, e = args
        gu = x_pad[tok] @ w_gu[e] + b_gu[e]
        gate, up = jnp.split(gu, 2, axis=-1)
        gate = jnp.minimum(gate, SWIGLU_LIMIT)
        up = jnp.clip(up, -SWIGLU_LIMIT, SWIGLU_LIMIT)
        glu = gate * jax.nn.sigmoid(SWIGLU_ALPHA * gate)
        return ((up + 1.0) * glu) @ w_down[e] + b_down[e]

    y = lax.map(expert_block, (slot_tok.reshape(n_blocks, MOE_BLOCK), block_expert))
    y = y.reshape(n_slots, D) * slot_gate[:, None].astype(y.dtype)
    out = jax.ops.segment_sum(y, slot_tok, num_segments=n + 1)[:n]
    return out.reshape(B, T, D)


def trunk_layer(x, cond, lp, ctx):
    mod = jnp.einsum('...d,de->...e', jax.nn.silu(cond), lp['mod_w']) + lp['mod_b']
    mod = mod.reshape(-1, 1, mod.shape[-1])
    sh1, sc1, g1, sh2, sc2, g2 = jnp.split(mod, 6, axis=-1)
    mix, states = token_mix(layer_norm(x) * (1.0 + sc1) + sh1, lp, ctx)
    x = affine_layer_norm(DEEPNORM_ALPHA * x + g1 * mix, lp['ln1_g'], lp['ln1_b'])
    ffn = moe_ffn(layer_norm(x) * (1.0 + sc2) + sh2, lp)
    x = affine_layer_norm(DEEPNORM_ALPHA * x + g2 * ffn, lp['ln2_g'], lp['ln2_b'])
    return x, states


def setup_inputs(seed: int = 0) -> dict:
    key = jax.random.key(seed)
    ks = jax.random.split(key, 40)
    f32 = jnp.float32

    def nrm(k, shape, s):
        return jax.random.normal(k, shape, f32) * s

    D = D_MODEL
    return {
        'x_prompt': nrm(ks[0], (BATCH, SEQ, D), 1.0),
        'x_sample': nrm(ks[1], (DEC_BATCH, DEC_SEQ, D), 1.0),
        'state_gla': nrm(ks[2], (DEC_BATCH, DEPTH, 2, GLA_HEADS, GLA_DK, GLA_DV), 0.3),
        'state_ret': nrm(ks[3], (DEC_BATCH, DEPTH, 2, RET_HEADS, RET_DK, RET_DV), 0.3),
        'cache_gqa_k': nrm(ks[4], (DEC_BATCH, DEPTH, PAST_LEN, GQA_KV_HEADS, GQA_DIM), 1.0),
        'cache_gqa_v': nrm(ks[5], (DEC_BATCH, DEPTH, PAST_LEN, GQA_KV_HEADS, GQA_DIM), 1.0),
        'cache_nat_k': nrm(ks[6], (DEC_BATCH, DEPTH, PAST_LEN, NAT_HEADS, NAT_DIM), 1.0),
        'cache_nat_v': nrm(ks[7], (DEC_BATCH, DEPTH, PAST_LEN, NAT_HEADS, NAT_DIM), 1.0),
        'c': nrm(ks[8], (DEC_BATCH, D), 1.0),
        'c_ctx': nrm(ks[9], (D,), 1.0),
        'mod_w': nrm(ks[10], (DEPTH, D, 6 * D), 0.5 * D ** -0.5),
        'mod_b': nrm(ks[11], (DEPTH, 6 * D), 0.02),
        'w_in': nrm(ks[12], (DEPTH, D, PROJ_WIDTH), D ** -0.5),
        'gla_lr_up': nrm(ks[13], (DEPTH, 2, GLA_RANK, GLA_HEADS * GLA_DK), GLA_RANK ** -0.5),
        'gla_lr_b': nrm(ks[14], (DEPTH, 2, GLA_HEADS * GLA_DK), 0.1),
        'gla_norm_g': 1.0 + nrm(ks[15], (DEPTH, GLA_DV), 0.02),
        'ret_gn_g': 1.0 + nrm(ks[16], (DEPTH, RET_HEADS * RET_DV), 0.02),
        'ret_gn_b': nrm(ks[17], (DEPTH, RET_HEADS * RET_DV), 0.02),
        'gqa_qn_g': 1.0 + nrm(ks[18], (DEPTH, GQA_DIM), 0.02),
        'gqa_kn_g': 1.0 + nrm(ks[19], (DEPTH, GQA_DIM), 0.02),
        'nat_rpb': nrm(ks[20], (DEPTH, NAT_HEADS, 2 * NAT_ROWS - 1, 2 * NAT_COLS - 1), 0.1),
        'w_o': nrm(ks[21], (DEPTH, MIX_WIDTH, D), DEEPNORM_BETA * MIX_WIDTH ** -0.5),
        'ln1_g': 1.0 + nrm(ks[22], (DEPTH, D), 0.02),
        'ln1_b': nrm(ks[23], (DEPTH, D), 0.02),
        'router_w': nrm(ks[24], (DEPTH, D, N_EXPERTS), D ** -0.5),
        'router_b': nrm(ks[25], (DEPTH, N_EXPERTS), 0.01),
        'w_gu': nrm(ks[26], (DEPTH, N_EXPERTS, D, 2 * EXPERT_FF), D ** -0.5),
        'b_gu': nrm(ks[27], (DEPTH, N_EXPERTS, 2 * EXPERT_FF), 0.02),
        'w_down': nrm(ks[28], (DEPTH, N_EXPERTS, EXPERT_FF, D), DEEPNORM_BETA * EXPERT_FF ** -0.5),
        'b_down': nrm(ks[29], (DEPTH, N_EXPERTS, D), 0.02),
        'ln2_g': 1.0 + nrm(ks[30], (DEPTH, D), 0.02),
        'ln2_b': nrm(ks[31], (DEPTH, D), 0.02),
    }


def reference(x_prompt, x_sample, state_gla, state_ret, cache_gqa_k, cache_gqa_v, cache_nat_k, cache_nat_v,
              c, c_ctx, mod_w, mod_b, w_in, gla_lr_up, gla_lr_b, gla_norm_g, ret_gn_g, ret_gn_b,
              gqa_qn_g, gqa_kn_g, nat_rpb, w_o, ln1_g, ln1_b, router_w, router_b, w_gu, b_gu,
              w_down, b_down, ln2_g, ln2_b):
    y_p = x_prompt
    y_s = x_sample
    new_gla, new_ret, new_gk, new_gv, new_nk, new_nv = [], [], [], [], [], []
    for l in range(DEPTH):
        lp = {'mod_w': mod_w[l], 'mod_b': mod_b[l], 'w_in': w_in[l], 'gla_lr_up': gla_lr_up[l],
              'gla_lr_b': gla_lr_b[l], 'gla_norm_g': gla_norm_g[l], 'ret_gn_g': ret_gn_g[l],
              'ret_gn_b': ret_gn_b[l], 'gqa_qn_g': gqa_qn_g[l], 'gqa_kn_g': gqa_kn_g[l],
              'nat_rpb': nat_rpb[l], 'w_o': w_o[l], 'ln1_g': ln1_g[l], 'ln1_b': ln1_b[l],
              'router_w': router_w[l], 'router_b': router_b[l], 'w_gu': w_gu[l], 'b_gu': b_gu[l],
              'w_down': w_down[l], 'b_down': b_down[l], 'ln2_g': ln2_g[l], 'ln2_b': ln2_b[l]}
        y_p, st = trunk_layer(y_p, c_ctx, lp, None)
        new_gla.append(st[0])
        new_ret.append(st[1])
        new_gk.append(st[2])
        new_gv.append(st[3])
        new_nk.append(st[4])
        new_nv.append(st[5])
        cache = (state_gla[:, l], state_ret[:, l], cache_gqa_k[:, l], cache_gqa_v[:, l],
                 cache_nat_k[:, l], cache_nat_v[:, l])
        y_s, _ = trunk_layer(y_s, c, lp, cache)
    return (y_p, y_s, jnp.stack(new_gla, axis=1), jnp.stack(new_ret, axis=1), jnp.stack(new_gk, axis=1),
            jnp.stack(new_gv, axis=1), jnp.stack(new_nk, axis=1), jnp.stack(new_nv, axis=1))
```

```python
import functools

import numpy as np
import jax
import jax.numpy as jnp
from jax import lax
from jax.experimental import pallas as pl
from jax.experimental.pallas import tpu as pltpu

F32 = jnp.float32
BF16 = jnp.bfloat16

D_MODEL = 4096
BATCH = 16
SEQ = 256
DEPTH = 2
DEC_BATCH = 8
DEC_SEQ = 1024
PAST_LEN = 512
GRID_W = 64
GLA_HEADS = 8
GLA_DK = 64
GLA_DV = 128
GLA_RANK = 16
GLA_NORMALIZER = 16.0
RET_HEADS = 8
RET_DK = 64
RET_DV = 128
GQA_HEADS = 8
GQA_KV_HEADS = 2
GQA_DIM = 128
NAT_HEADS = 8
NAT_DIM = 128
NAT_ROWS = 8
NAT_COLS = 16
CHUNK = 64
ROPE_THETA = 10000.0
LN_EPS = 1e-6
N_EXPERTS = 32
TOP_K = 4
EXPERT_FF = 2048
SWIGLU_LIMIT = 7.0
SWIGLU_ALPHA = 1.702
DEEPNORM_ALPHA = (2.0 * DEPTH) ** 0.25

N_PROMPT = BATCH * SEQ
N_SAMPLE = DEC_BATCH * DEC_SEQ
N_TOK = N_PROMPT + N_SAMPLE

LANES = 128
ROW_TILE = 256
NEG_BIG = -1e30

COL_A_Q, COL_A_K, COL_A_V, COL_A_G = 0, 512, 1024, 2048
COL_B_Q, COL_B_K, COL_B_V, COL_B_G = 3072, 3584, 4096, 5120
COL_C_Q, COL_C_K, COL_C_V = 6144, 7168, 7424
COL_D_Q, COL_D_K, COL_D_V = 7680, 8704, 9728
COL_A_LR = 10752
PROJ_PAD = 10880
ALR_SRC = 3072
ALR_W = 2 * GLA_RANK

MOE_TM = 256
NK = N_TOK * TOP_K
MOE_TILES = NK // MOE_TM + N_EXPERTS
MOE_SLOTS = MOE_TILES * MOE_TM
MOE_TF = 512
MOE_TN = 1024
ROW_PAD = D_MODEL + LANES


def _cparams(sem, vmem_mb):
    return pltpu.CompilerParams(dimension_semantics=sem, vmem_limit_bytes=int(vmem_mb * 1024 * 1024))


def _dot(a, b):
    return jnp.dot(a, b, preferred_element_type=F32)


def _dot_nt(a, b):
    return lax.dot_general(a, b, (((1,), (1,)), ((), ())), preferred_element_type=F32)


def _dot_tn(a, b):
    return lax.dot_general(a, b, (((0,), (0,)), ((), ())), preferred_element_type=F32)


def _sigmoid(x):
    return 1.0 / (1.0 + jnp.exp(-x))


def _silu(x):
    return x * _sigmoid(x)


def _ln_rows(x):
    mu = jnp.mean(x, axis=-1, keepdims=True)
    xc = x - mu
    var = jnp.mean(xc * xc, axis=-1, keepdims=True)
    return xc * lax.rsqrt(var + LN_EPS)


def _rms_rows(x):
    return x * lax.rsqrt(jnp.mean(x * x, axis=-1, keepdims=True) + LN_EPS)


def _mod_row(i):
    n_ctx_tiles = N_PROMPT // ROW_TILE
    per_b = DEC_SEQ // ROW_TILE
    return jnp.where(i < n_ctx_tiles, 0, 1 + (i - n_ctx_tiles) // per_b)


def _mm_kernel(a_ref, w_ref, b_ref, o_ref, *, silu_a):
    a = a_ref[...]
    if silu_a:
        a = _silu(a.astype(F32))
    acc = _dot(a.astype(BF16), w_ref[...].astype(BF16))
    o_ref[...] = acc + b_ref[...]


def _matmul(a, w, bias, *, tm, tn, silu_a=False, vmem_mb, name):
    m, k = a.shape
    n = w.shape[1]
    return pl.pallas_call(
        functools.partial(_mm_kernel, silu_a=silu_a),
        grid=(m // tm, n // tn),
        in_specs=[pl.BlockSpec((tm, k), lambda i, j: (i, 0)),
                  pl.BlockSpec((k, tn), lambda i, j: (0, j)),
                  pl.BlockSpec((1, tn), lambda i, j: (0, j))],
        out_specs=pl.BlockSpec((tm, tn), lambda i, j: (i, j)),
        out_shape=jax.ShapeDtypeStruct((m, n), F32),
        compiler_params=_cparams(("arbitrary", "arbitrary"), vmem_mb),
        name=name)(a, w, bias)


def _mm4_kernel(a0, a1, a2, a3, w0, w1, w2, w3, o_ref):
    acc = _dot(a0[...], w0[...].astype(BF16))
    acc += _dot(a1[...], w1[...].astype(BF16))
    acc += _dot(a2[...], w2[...].astype(BF16))
    acc += _dot(a3[...], w3[...].astype(BF16))
    o_ref[...] = acc


def _out_proj(ys, w_o, *, tm=1024, tn=512):
    kp = ys[0].shape[1]
    a_specs = [pl.BlockSpec((tm, kp), lambda i, j: (i, 0)) for _ in range(4)]
    w_specs = [pl.BlockSpec((kp, tn), functools.partial(lambda i, j, p: (p, j), p=p)) for p in range(4)]
    return pl.pallas_call(
        _mm4_kernel,
        grid=(N_TOK // tm, D_MODEL // tn),
        in_specs=a_specs + w_specs,
        out_specs=pl.BlockSpec((tm, tn), lambda i, j: (i, j)),
        out_shape=jax.ShapeDtypeStruct((N_TOK, D_MODEL), F32),
        compiler_params=_cparams(("arbitrary", "arbitrary"), 48),
        name="out_proj")(*ys, w_o, w_o, w_o, w_o)


def _ln_mod_kernel(x_ref, sh_ref, sc_ref, h_ref):
    h = _ln_rows(x_ref[...]) * (1.0 + sc_ref[0]) + sh_ref[0]
    h_ref[...] = h.astype(BF16)


def _mod_spec(which):
    return pl.BlockSpec((1, 1, D_MODEL), lambda i, *_: (_mod_row(i) * 6 + which, 0, 0))


def _ln_mod(x, mod3):
    return pl.pallas_call(
        _ln_mod_kernel,
        grid=(N_TOK // ROW_TILE,),
        in_specs=[pl.BlockSpec((ROW_TILE, D_MODEL), lambda i: (i, 0)), _mod_spec(0), _mod_spec(1)],
        out_specs=pl.BlockSpec((ROW_TILE, D_MODEL), lambda i: (i, 0)),
        out_shape=jax.ShapeDtypeStruct((N_TOK, D_MODEL), BF16),
        compiler_params=_cparams(("arbitrary",), 32),
        name="ln_mod")(x, mod3, mod3)


def _mod_spec_tile(which, tile):
    n_ctx_tiles = N_PROMPT // tile
    per_b = DEC_SEQ // tile

    def imap(i, *_):
        row = jnp.where(i < n_ctx_tiles, 0, 1 + (i - n_ctx_tiles) // per_b)
        return (row * 6 + which, 0, 0)
    return pl.BlockSpec((1, 1, D_MODEL), imap)


def _res_ln1_kernel(x_ref, mix_ref, gate_ref, g_ref, b_ref, sh_ref, sc_ref, x1_ref, h2_ref, rows_ref):
    y = DEEPNORM_ALPHA * x_ref[...] + gate_ref[0] * mix_ref[...]
    x1 = _ln_rows(y) * g_ref[...] + b_ref[...]
    x1_ref[...] = x1
    h2 = _ln_rows(x1) * (1.0 + sc_ref[0]) + sh_ref[0]
    h2_ref[...] = h2.astype(BF16)
    rows_ref[:, 0, 0:D_MODEL] = h2
    rows_ref[:, 0, D_MODEL:ROW_PAD] = jnp.zeros((x1.shape[0], ROW_PAD - D_MODEL), F32)


def _res_ln1(x, mix, mod3, ln_g, ln_b):
    tile = ROW_TILE
    row_spec = pl.BlockSpec((tile, D_MODEL), lambda i: (i, 0))
    vec_spec = pl.BlockSpec((1, D_MODEL), lambda i: (0, 0))
    return pl.pallas_call(
        _res_ln1_kernel,
        grid=(N_TOK // tile,),
        in_specs=[row_spec, row_spec, _mod_spec_tile(2, tile), vec_spec, vec_spec,
                  _mod_spec_tile(3, tile), _mod_spec_tile(4, tile)],
        out_specs=[row_spec, row_spec, pl.BlockSpec((tile, 1, ROW_PAD), lambda i: (i, 0, 0))],
        out_shape=[jax.ShapeDtypeStruct((N_TOK, D_MODEL), F32), jax.ShapeDtypeStruct((N_TOK, D_MODEL), BF16),
                   jax.ShapeDtypeStruct((N_TOK, 1, ROW_PAD), F32)],
        compiler_params=_cparams(("arbitrary",), 48),
        name="res_ln_mixer")(x, mix, mod3, ln_g, ln_b, mod3, mod3)


def _gather_step(idx_cur, idx_nxt, src_ref, buf, sems, n_rows):
    step = pl.program_id(0)
    slot = step % 2

    cols = pl.ds(0, D_MODEL)

    def issue(idx_ref, to_slot):
        def body(r, carry):
            pltpu.make_async_copy(src_ref.at[idx_ref[0, 0, r], :, cols], buf.at[to_slot, r, :, cols],
                                  sems.at[to_slot]).start()
            return carry
        lax.fori_loop(0, n_rows, body, 0)

    @pl.when(step == 0)
    def _():
        issue(idx_cur, slot)

    @pl.when(step + 1 < pl.num_programs(0))
    def _():
        issue(idx_nxt, 1 - slot)

    pltpu.make_async_copy(src_ref.at[pl.ds(0, n_rows), :, cols], buf.at[slot, :, :, cols], sems.at[slot]).wait()
    return slot


def _idx_specs(n_rows, n_steps):
    return [pl.BlockSpec((1, 1, n_rows), lambda i: (i, 0, 0), memory_space=pltpu.SMEM),
            pl.BlockSpec((1, 1, n_rows), lambda i: (jnp.minimum(i + 1, n_steps - 1), 0, 0),
                         memory_space=pltpu.SMEM)]


def _res_ln2_kernel(idx_cur, idx_nxt, x_ref, y_hbm, gate_ref, g_ref, b_ref, o_ref, buf, sems, *, tile):
    slot = _gather_step(idx_cur, idx_nxt, y_hbm, buf, sems, TOP_K * tile)
    mix = buf[slot, 0:tile, 0, 0:D_MODEL]
    for kk in range(1, TOP_K):
        mix = mix + buf[slot, kk * tile:(kk + 1) * tile, 0, 0:D_MODEL]
    y = DEEPNORM_ALPHA * x_ref[...] + gate_ref[0] * mix
    o_ref[...] = _ln_rows(y) * g_ref[...] + b_ref[...]


def _res_ln2(x1, y_rows, dest, mod3, ln_g, ln_b):
    tile = 64
    n_tiles = N_TOK // tile
    idx = dest.reshape(n_tiles, tile, TOP_K).transpose(0, 2, 1).reshape(n_tiles, 1, TOP_K * tile)
    row_spec = pl.BlockSpec((tile, D_MODEL), lambda i: (i, 0))
    vec_spec = pl.BlockSpec((1, D_MODEL), lambda i: (0, 0))
    return pl.pallas_call(
        functools.partial(_res_ln2_kernel, tile=tile),
        grid=(n_tiles,),
        in_specs=_idx_specs(TOP_K * tile, n_tiles) + [row_spec, pl.BlockSpec(memory_space=pl.ANY),
                                             _mod_spec_tile(5, tile), vec_spec, vec_spec],
        out_specs=row_spec,
        out_shape=jax.ShapeDtypeStruct((N_TOK, D_MODEL), F32),
        scratch_shapes=[pltpu.VMEM((2, TOP_K * tile, 1, ROW_PAD), F32), pltpu.SemaphoreType.DMA((2,))],
        compiler_params=_cparams(("arbitrary",), 40),
        name="res_ln_moe")(idx, idx, x1, y_rows, mod3, ln_g, ln_b)


def _gather_x_kernel(idx_cur, idx_nxt, src_hbm, o_ref, buf, sems):
    slot = _gather_step(idx_cur, idx_nxt, src_hbm, buf, sems, MOE_TM)
    o_ref[...] = buf[slot, :, 0, 0:D_MODEL].astype(BF16)


def _gather_x(h2_rows, slot_tok):
    idx = slot_tok.reshape(MOE_TILES, 1, MOE_TM)
    return pl.pallas_call(
        _gather_x_kernel,
        grid=(MOE_TILES,),
        in_specs=_idx_specs(MOE_TM, MOE_TILES) + [pl.BlockSpec(memory_space=pl.ANY)],
        out_specs=pl.BlockSpec((MOE_TM, D_MODEL), lambda i: (i, 0)),
        out_shape=jax.ShapeDtypeStruct((MOE_SLOTS, D_MODEL), BF16),
        scratch_shapes=[pltpu.VMEM((2, MOE_TM, 1, ROW_PAD), F32), pltpu.SemaphoreType.DMA((2,))],
        compiler_params=_cparams(("arbitrary",), 32),
        name="moe_gather_x")(idx, idx, h2_rows)


def _rope(x, cos, sin_signed, quarter):
    lane = lax.broadcasted_iota(jnp.int32, x.shape, 1)
    up = pltpu.roll(x, LANES - quarter, axis=1)
    dn = pltpu.roll(x, quarter, axis=1)
    partner = jnp.where((lane % (2 * quarter)) < quarter, up, dn)
    return x * cos + partner * sin_signed


def _rope_tables(t_len, quarter):
    lane = np.arange(LANES)
    within = lane % (4 * quarter)
    is_col = within >= 2 * quarter
    j = within % quarter
    second = (within % (2 * quarter)) >= quarter
    freqs = ROPE_THETA ** (-(j.astype(np.float32)) / quarter)
    t = np.arange(t_len)
    pos = np.where(is_col[None, :], (t % GRID_W)[:, None], (t // GRID_W)[:, None]).astype(np.float32)
    ang = pos * freqs[None, :].astype(np.float32)
    cos = np.cos(ang).astype(np.float32)
    sin = np.sin(ang).astype(np.float32) * np.where(second, 1.0, -1.0)[None, :].astype(np.float32)
    return jnp.asarray(cos), jnp.asarray(sin)


def _attend(q, keys, vals, scale, biases=None):
    ss = []
    for idx, k in enumerate(keys):
        s = _dot_nt(q, k) * scale
        if biases is not None and biases[idx] is not None:
            s = s + biases[idx]
        ss.append(s)
    m = ss[0].max(axis=-1, keepdims=True)
    for s in ss[1:]:
        m = jnp.maximum(m, s.max(axis=-1, keepdims=True))
    l = None
    o = None
    for s, v in zip(ss, vals):
        p = jnp.exp(s - m)
        pl_sum = p.sum(axis=-1, keepdims=True)
        po = _dot(p.astype(BF16), v)
        l = pl_sum if l is None else l + pl_sum
        o = po if o is None else o + po
    return o / l


def _ctx_attn_kernel(q_ref, k_ref, v_ref, gq_ref, gk_ref, o_ref, *rest, groups, norm):
    k = k_ref[...]
    if norm:
        k = _rms_rows(k) * gk_ref[...]
        rest[0][...] = k
    kb = k.astype(BF16)
    vb = v_ref[...].astype(BF16)
    scale = GQA_DIM ** -0.5
    for g in range(groups):
        q = q_ref[:, g * LANES:(g + 1) * LANES]
        if norm:
            q = _rms_rows(q) * gq_ref[...]
        o_ref[:, g * LANES:(g + 1) * LANES] = _attend(q.astype(BF16), [kb], [vb], scale).astype(BF16)


def _ctx_attn(proj, col_q, col_k, col_v, n_kv, groups, gq, gk, *, norm, name):
    qw = groups * LANES
    in_specs = [pl.BlockSpec((SEQ, qw), lambda b, h: (b, col_q // qw + h)),
                pl.BlockSpec((SEQ, LANES), lambda b, h: (b, col_k // LANES + h)),
                pl.BlockSpec((SEQ, LANES), lambda b, h: (b, col_v // LANES + h)),
                pl.BlockSpec((1, LANES), lambda b, h: (0, 0)),
                pl.BlockSpec((1, LANES), lambda b, h: (0, 0))]
    out_specs = [pl.BlockSpec((SEQ, qw), lambda b, h: (b, h))]
    out_shape = [jax.ShapeDtypeStruct((N_PROMPT, n_kv * qw), BF16)]
    if norm:
        out_specs.append(pl.BlockSpec((SEQ, LANES), lambda b, h: (b, h)))
        out_shape.append(jax.ShapeDtypeStruct((N_PROMPT, n_kv * LANES), F32))
    return pl.pallas_call(
        functools.partial(_ctx_attn_kernel, groups=groups, norm=norm),
        grid=(BATCH, n_kv),
        in_specs=in_specs, out_specs=out_specs, out_shape=out_shape,
        compiler_params=_cparams(("arbitrary", "arbitrary"), 32),
        name=name)(proj, proj, proj, gq, gk)


def _gqa_lat_kernel(q_ref, k_ref, v_ref, ck_ref, cv_ref, gq_ref, gk_ref, cosq_ref, sinq_ref, cosk_ref, sink_ref,
                    o_ref, k_scr, v_scr):
    qb = pl.program_id(2)

    @pl.when(qb == 0)
    def _():
        k = _rms_rows(k_ref[...]) * gk_ref[...]
        k_scr[...] = _rope(k, cosk_ref[...], sink_ref[...], GQA_DIM // 4).astype(BF16)
        v_scr[...] = v_ref[...].astype(BF16)

    ck = ck_ref[...].astype(BF16)
    cv = cv_ref[...].astype(BF16)
    scale = GQA_DIM ** -0.5
    for g in range(GQA_HEADS // GQA_KV_HEADS):
        q = _rms_rows(q_ref[:, g * LANES:(g + 1) * LANES]) * gq_ref[...]
        q = _rope(q, cosq_ref[...], sinq_ref[...], GQA_DIM // 4).astype(BF16)
        o = _attend(q, [k_scr[...], ck], [v_scr[...], cv], scale)
        o_ref[:, g * LANES:(g + 1) * LANES] = o.astype(BF16)


def _gqa_lat(proj, cache_k, cache_v, layer, gq, gk, cos, sin):
    groups = GQA_HEADS // GQA_KV_HEADS
    qw = groups * LANES
    tq = ROW_TILE
    nqb = DEC_SEQ // tq
    row0 = N_PROMPT // tq
    in_specs = [
        pl.BlockSpec((tq, qw), lambda b, h, i: (row0 + b * nqb + i, COL_C_Q // qw + h)),
        pl.BlockSpec((DEC_SEQ, LANES), lambda b, h, i: (N_PROMPT // DEC_SEQ + b, COL_C_K // LANES + h)),
        pl.BlockSpec((DEC_SEQ, LANES), lambda b, h, i: (N_PROMPT // DEC_SEQ + b, COL_C_V // LANES + h)),
        pl.BlockSpec((None, None, PAST_LEN, LANES), lambda b, h, i: (b, layer, 0, h)),
        pl.BlockSpec((None, None, PAST_LEN, LANES), lambda b, h, i: (b, layer, 0, h)),
        pl.BlockSpec((1, LANES), lambda b, h, i: (0, 0)),
        pl.BlockSpec((1, LANES), lambda b, h, i: (0, 0)),
        pl.BlockSpec((tq, LANES), lambda b, h, i: (i, 0)),
        pl.BlockSpec((tq, LANES), lambda b, h, i: (i, 0)),
        pl.BlockSpec((DEC_SEQ, LANES), lambda b, h, i: (0, 0)),
        pl.BlockSpec((DEC_SEQ, LANES), lambda b, h, i: (0, 0)),
    ]
    return pl.pallas_call(
        _gqa_lat_kernel,
        grid=(DEC_BATCH, GQA_KV_HEADS, nqb),
        in_specs=in_specs,
        out_specs=pl.BlockSpec((tq, qw), lambda b, h, i: (b * nqb + i, h)),
        out_shape=jax.ShapeDtypeStruct((N_SAMPLE, GQA_HEADS * GQA_DIM), BF16),
        scratch_shapes=[pltpu.VMEM((DEC_SEQ, LANES), BF16), pltpu.VMEM((DEC_SEQ, LANES), BF16)],
        compiler_params=_cparams(("arbitrary", "arbitrary", "arbitrary"), 40),
        name="gqa_latent")(proj, proj, proj, cache_k, cache_v, gq, gk, cos, sin, cos, sin)


NAT_GRID_ROWS = DEC_SEQ // GRID_W
NAT_KR = min(NAT_ROWS, NAT_GRID_ROWS)
NAT_BAND = NAT_KR * GRID_W
NAT_PATTERNS = NAT_KR


def _nat_lat_kernel(q_ref, k_ref, v_ref, ck_ref, cv_ref, bias_ref, o_ref, k_scr, v_scr, ck_scr, cv_scr):
    k_scr[...] = k_ref[...].astype(BF16)
    v_scr[...] = v_ref[...].astype(BF16)
    ck_scr[...] = ck_ref[...].astype(BF16)
    cv_scr[...] = cv_ref[...].astype(BF16)
    scale = NAT_DIM ** -0.5

    def body(r, carry):
        band = jnp.clip(r - NAT_KR // 2, 0, NAT_GRID_ROWS - NAT_KR)
        start = pl.multiple_of(band * GRID_W, GRID_W)
        rows = pl.ds(pl.multiple_of(r * GRID_W, GRID_W), GRID_W)
        q = q_ref[rows, :].astype(BF16)
        kb = k_scr[pl.ds(start, NAT_BAND), :]
        vb = v_scr[pl.ds(start, NAT_BAND), :]
        bias = bias_ref[r - band]
        o = _attend(q, [kb, ck_scr[...]], [vb, cv_scr[...]], scale, biases=[bias, None])
        o_ref[rows, :] = o.astype(BF16)
        return carry

    lax.fori_loop(0, NAT_GRID_ROWS, body, 0)


def _nat_lat(proj, cache_k, cache_v, layer, bias_tab):
    rb = N_PROMPT // DEC_SEQ
    in_specs = [
        pl.BlockSpec((DEC_SEQ, LANES), lambda b, h: (rb + b, COL_D_Q // LANES + h)),
        pl.BlockSpec((DEC_SEQ, LANES), lambda b, h: (rb + b, COL_D_K // LANES + h)),
        pl.BlockSpec((DEC_SEQ, LANES), lambda b, h: (rb + b, COL_D_V // LANES + h)),
        pl.BlockSpec((None, None, PAST_LEN, LANES), lambda b, h: (b, layer, 0, h)),
        pl.BlockSpec((None, None, PAST_LEN, LANES), lambda b, h: (b, layer, 0, h)),
        pl.BlockSpec((None, NAT_PATTERNS, GRID_W, NAT_BAND), lambda b, h: (h, 0, 0, 0)),
    ]
    return pl.pallas_call(
        _nat_lat_kernel,
        grid=(DEC_BATCH, NAT_HEADS),
        in_specs=in_specs,
        out_specs=pl.BlockSpec((DEC_SEQ, LANES), lambda b, h: (b, h)),
        out_shape=jax.ShapeDtypeStruct((N_SAMPLE, NAT_HEADS * NAT_DIM), BF16),
        scratch_shapes=[pltpu.VMEM((DEC_SEQ, LANES), BF16), pltpu.VMEM((DEC_SEQ, LANES), BF16),
                        pltpu.VMEM((PAST_LEN, LANES), BF16), pltpu.VMEM((PAST_LEN, LANES), BF16)],
        compiler_params=_cparams(("arbitrary", "arbitrary"), 32),
        name="nat_latent")(proj, proj, proj, cache_k, cache_v, bias_tab)


def _nat_bias_table(rpb):
    off = np.arange(NAT_PATTERNS)
    kr = np.arange(NAT_KR)
    rel_r = kr[None, :] - off[:, None] + NAT_ROWS - 1
    c = np.arange(GRID_W)
    rel_c = np.clip(c[None, :] - c[:, None], 1 - NAT_COLS, NAT_COLS - 1) + NAT_COLS - 1
    col_start = np.clip(c - NAT_COLS // 2, 0, GRID_W - NAT_COLS)
    col_ok = (c[None, :] >= col_start[:, None]) & (c[None, :] < col_start[:, None] + NAT_COLS)
    bias = rpb[:, rel_r[:, None, :, None], rel_c[None, :, None, :]]
    bias = jnp.where(jnp.asarray(col_ok)[None, None, :, None, :], bias.astype(F32), NEG_BIG)
    return bias.reshape(NAT_HEADS, NAT_PATTERNS, GRID_W, NAT_BAND)


def _ret_log_decays():
    expo = -5.0 - np.arange(RET_HEADS, dtype=np.float64)
    return np.log1p(-np.exp2(expo))


def _ret_kernel(lg_ref, q_ref, k_ref, v_ref, g_ref, gng_ref, gnb_ref, *rest, t_len, latent):
    if latent:
        cos_ref, sin_ref, s0_ref, y_ref = rest
    else:
        y_ref, st_ref = rest
    p = pl.program_id(1)
    q = q_ref[...]
    k = k_ref[...] * (RET_DK ** -0.5)
    if latent:
        q = _rope(q, cos_ref[...], sin_ref[...], RET_DK // 4)
        k = _rope(k, cos_ref[...], sin_ref[...], RET_DK // 4)
    kb = k.astype(BF16)
    lane = lax.broadcasted_iota(jnp.int32, (t_len, LANES), 1)
    tq = ROW_TILE
    for h in range(2):
        lgf = lg_ref[2 * p + h]
        lgb = lg_ref[RET_HEADS - 1 - (2 * p + h)]
        head_lanes = (lane >= h * RET_DK) & (lane < (h + 1) * RET_DK)
        qh = jnp.where(head_lanes, q, 0.0).astype(BF16)
        vh = v_ref[:, h * RET_DV:(h + 1) * RET_DV].astype(BF16)
        for blk in range(t_len // tq):
            r0 = blk * tq
            i = lax.broadcasted_iota(jnp.int32, (tq, t_len), 0) + r0
            j = lax.broadcasted_iota(jnp.int32, (tq, t_len), 1)
            diff = (i - j).astype(F32)
            arg = jnp.where(diff > 0, lgf * diff, -lgb * diff)
            dmat = jnp.exp(arg) * jnp.where(diff == 0, 2.0, 1.0)
            qblk = qh[r0:r0 + tq]
            s = _dot_nt(qblk, kb) * dmat
            o = _dot(s.astype(BF16), vh)
            if latent:
                pos = (lax.broadcasted_iota(jnp.int32, (tq, 1), 0) + r0).astype(F32)
                o = o + _dot(qblk, s0_ref[0].astype(BF16)) * jnp.exp(lgf * (pos + 1.0))
                o = o + _dot(qblk, s0_ref[1].astype(BF16)) * jnp.exp(lgb * (t_len - pos))
            y = _ln_rows(o) * gng_ref[:, h * RET_DV:(h + 1) * RET_DV] + gnb_ref[:, h * RET_DV:(h + 1) * RET_DV]
            gate = g_ref[r0:r0 + tq, h * RET_DV:(h + 1) * RET_DV]
            y_ref[r0:r0 + tq, h * RET_DV:(h + 1) * RET_DV] = (y * _silu(gate)).astype(BF16)
        if not latent:
            pos = lax.broadcasted_iota(jnp.int32, (t_len, 1), 0).astype(F32)
            kf = (k * jnp.exp(lgf * (t_len - 1.0 - pos))).astype(BF16)
            kbw = (k * jnp.exp(lgb * pos)).astype(BF16)
            st_ref[0, h] = _dot_tn(kf, vh)[h * RET_DK:(h + 1) * RET_DK, :]
            st_ref[1, h] = _dot_tn(kbw, vh)[h * RET_DK:(h + 1) * RET_DK, :]


def _ret(proj, lg, gn_g, gn_b, *, latent, cos=None, sin=None, state=None, layer=None):
    t_len = DEC_SEQ if latent else SEQ
    nb = DEC_BATCH if latent else BATCH
    rb = N_PROMPT // DEC_SEQ if latent else 0
    pw = 2 * RET_DV
    in_specs = [
        pl.BlockSpec(memory_space=pltpu.SMEM),
        pl.BlockSpec((t_len, LANES), lambda b, p: (rb + b, COL_B_Q // LANES + p)),
        pl.BlockSpec((t_len, LANES), lambda b, p: (rb + b, COL_B_K // LANES + p)),
        pl.BlockSpec((t_len, pw), lambda b, p: (rb + b, COL_B_V // pw + p)),
        pl.BlockSpec((t_len, pw), lambda b, p: (rb + b, COL_B_G // pw + p)),
        pl.BlockSpec((1, pw), lambda b, p: (0, p)),
        pl.BlockSpec((1, pw), lambda b, p: (0, p)),
    ]
    args = [lg, proj, proj, proj, proj, gn_g, gn_b]
    out_specs = [pl.BlockSpec((t_len, pw), lambda b, p: (b, p))]
    out_shape = [jax.ShapeDtypeStruct((nb * t_len, RET_HEADS * RET_DV), BF16)]
    if latent:
        in_specs += [pl.BlockSpec((t_len, LANES), lambda b, p: (0, 0)),
                     pl.BlockSpec((t_len, LANES), lambda b, p: (0, 0)),
                     pl.BlockSpec((None, None, 2, None, 2 * RET_DK, RET_DV), lambda b, p: (b, layer, 0, p, 0, 0))]
        args += [cos, sin, state]
    else:
        out_specs.append(pl.BlockSpec((None, 2, 2, RET_DK, RET_DV), lambda b, p: (b, 0, p, 0, 0)))
        out_shape.append(jax.ShapeDtypeStruct((nb, 2, RET_HEADS, RET_DK, RET_DV), F32))
    return pl.pallas_call(
        functools.partial(_ret_kernel, t_len=t_len, latent=latent),
        grid=(nb, RET_HEADS // 2),
        in_specs=in_specs, out_specs=out_specs, out_shape=out_shape,
        compiler_params=_cparams(("arbitrary", "arbitrary"), 48),
        name="ret_latent" if latent else "ret_ctx")(*args)


def _gla_kernel(q_ref, k_ref, v_ref, g_ref, lr_ref, up_ref, ub_ref, ng_ref, *rest, t_len, latent):
    if latent:
        s0_ref, y_ref, la_scr, o_scr, st_scr = rest
    else:
        y_ref, stout_ref, la_scr, o_scr, st_scr = rest
    n_chunks = t_len // CHUNK
    lr = lr_ref[...].astype(BF16)
    for d in range(2):
        z = _dot(lr, up_ref[d].astype(BF16)) + ub_ref[d]
        la_scr[d] = (jnp.minimum(z, 0.0) - jnp.log(1.0 + jnp.exp(-jnp.abs(z)))) / GLA_NORMALIZER

    ci = lax.broadcasted_iota(jnp.int32, (CHUNK, CHUNK), 0)
    cj = lax.broadcasted_iota(jnp.int32, (CHUNK, CHUNK), 1)
    lane = lax.broadcasted_iota(jnp.int32, (CHUNK, LANES), 1)
    head_lanes = [(lane >= h * GLA_DK) & (lane < (h + 1) * GLA_DK) for h in range(2)]

    for d in range(2):
        keep = (cj <= ci) if d == 0 else (cj >= ci)
        tri = jnp.where(keep, 1.0, 0.0).astype(BF16)
        if latent:
            s0t = s0_ref[d].T
            st_scr[0] = s0t
            st_scr[1] = s0t
        else:
            st_scr[0] = jnp.zeros((GLA_DV, LANES), F32)
            st_scr[1] = jnp.zeros((GLA_DV, LANES), F32)

        def body(step, carry, d=d, keep=keep, tri=tri):
            c = step if d == 0 else n_chunks - 1 - step
            rows = pl.ds(pl.multiple_of(c * CHUNK, CHUNK), CHUNK)
            la = la_scr[d, rows, :]
            la_hi = la.astype(BF16)
            la_lo = (la - la_hi.astype(F32)).astype(BF16)
            bcum = _dot(tri, la_hi) + _dot(tri, la_lo)
            btot = bcum[CHUNK - 1:CHUNK, :] if d == 0 else bcum[0:1, :]
            q_in = q_ref[rows, :] * (GLA_DK ** -0.5) * jnp.exp(bcum)
            kk = k_ref[rows, :]
            k_in = (kk * jnp.exp(-bcum)).astype(BF16)
            k_out = (kk * jnp.exp(btot - bcum)).astype(BF16)
            dec = jnp.exp(btot)
            for h in range(2):
                qm = jnp.where(head_lanes[h], q_in, 0.0).astype(BF16)
                vh = v_ref[rows, h * GLA_DV:(h + 1) * GLA_DV].astype(BF16)
                sc = jnp.where(keep, _dot_nt(qm, k_in), 0.0).astype(BF16)
                st = st_scr[h]
                o = _dot(sc, vh) + _dot_nt(qm, st.astype(BF16))
                st_scr[h] = st * dec + _dot_tn(vh, k_out)
                if d == 0:
                    o_scr[rows, h * GLA_DV:(h + 1) * GLA_DV] = o
                else:
                    o_scr[rows, h * GLA_DV:(h + 1) * GLA_DV] += o
            return carry

        lax.fori_loop(0, n_chunks, body, 0)
        if not latent:
            for h in range(2):
                stout_ref[d, h] = st_scr[h].T[h * GLA_DK:(h + 1) * GLA_DK, :]

    for h in range(2):
        cols = slice(h * GLA_DV, (h + 1) * GLA_DV)
        y = _rms_rows(o_scr[:, cols]) * ng_ref[...]
        y_ref[:, cols] = (y.astype(F32) * _silu(g_ref[:, cols])).astype(BF16)


def _gla(proj, up_mat, up_bias, norm_g, *, latent, state=None, layer=None):
    t_len = DEC_SEQ if latent else SEQ
    nb = DEC_BATCH if latent else BATCH
    rb = N_PROMPT // DEC_SEQ if latent else 0
    pw = 2 * GLA_DV
    in_specs = [
        pl.BlockSpec((t_len, LANES), lambda b, p: (rb + b, COL_A_Q // LANES + p)),
        pl.BlockSpec((t_len, LANES), lambda b, p: (rb + b, COL_A_K // LANES + p)),
        pl.BlockSpec((t_len, pw), lambda b, p: (rb + b, COL_A_V // pw + p)),
        pl.BlockSpec((t_len, pw), lambda b, p: (rb + b, COL_A_G // pw + p)),
        pl.BlockSpec((t_len, LANES), lambda b, p: (rb + b, COL_A_LR // LANES)),
        pl.BlockSpec((2, LANES, LANES), lambda b, p: (0, 0, p)),
        pl.BlockSpec((2, 1, LANES), lambda b, p: (0, 0, p)),
        pl.BlockSpec((1, GLA_DV), lambda b, p: (0, 0)),
    ]
    args = [proj, proj, proj, proj, proj, up_mat, up_bias, norm_g]
    out_specs = [pl.BlockSpec((t_len, pw), lambda b, p: (b, p))]
    out_shape = [jax.ShapeDtypeStruct((nb * t_len, GLA_HEADS * GLA_DV), BF16)]
    if latent:
        in_specs.append(pl.BlockSpec((None, None, 2, None, 2 * GLA_DK, GLA_DV),
                                     lambda b, p: (b, layer, 0, p, 0, 0)))
        args.append(state)
    else:
        out_specs.append(pl.BlockSpec((None, 2, 2, GLA_DK, GLA_DV), lambda b, p: (b, 0, p, 0, 0)))
        out_shape.append(jax.ShapeDtypeStruct((nb, 2, GLA_HEADS, GLA_DK, GLA_DV), F32))
    scratch = [pltpu.VMEM((2, t_len, LANES), F32), pltpu.VMEM((t_len, pw), F32),
               pltpu.VMEM((2, GLA_DV, LANES), F32)]
    return pl.pallas_call(
        functools.partial(_gla_kernel, t_len=t_len, latent=latent),
        grid=(nb, GLA_HEADS // 2),
        in_specs=in_specs, out_specs=out_specs, out_shape=out_shape, scratch_shapes=scratch,
        compiler_params=_cparams(("arbitrary", "arbitrary"), 32),
        name="gla_latent" if latent else "gla_ctx")(*args)


def _router_kernel(h_ref, w_ref, b_ref, idx_ref, gate_ref):
    logits = _dot(h_ref[...], w_ref[...].astype(BF16)) + b_ref[...]
    lane = lax.broadcasted_iota(jnp.int32, logits.shape, 1)
    idx_out = jnp.zeros(logits.shape, jnp.int32)
    val_out = jnp.zeros(logits.shape, F32)
    m0 = None
    denom = None
    for kk in range(TOP_K):
        m = logits.max(axis=-1, keepdims=True)
        sel = jnp.min(jnp.where(logits == m, lane, LANES), axis=-1, keepdims=True)
        if kk == 0:
            m0 = m
        e = jnp.exp(m - m0)
        denom = e if denom is None else denom + e
        idx_out = jnp.where(lane == kk, sel, idx_out)
        val_out = jnp.where(lane == kk, e, val_out)
        logits = jnp.where(lane == sel, -jnp.inf, logits)
    idx_ref[...] = idx_out
    gate_ref[...] = val_out / denom


def _router(h2, w_pad, b_pad):
    tm = 512
    return pl.pallas_call(
        _router_kernel,
        grid=(N_TOK // tm,),
        in_specs=[pl.BlockSpec((tm, D_MODEL), lambda i: (i, 0)),
                  pl.BlockSpec((D_MODEL, LANES), lambda i: (0, 0)),
                  pl.BlockSpec((1, LANES), lambda i: (0, 0))],
        out_specs=[pl.BlockSpec((tm, LANES), lambda i: (i, 0)), pl.BlockSpec((tm, LANES), lambda i: (i, 0))],
        out_shape=[jax.ShapeDtypeStruct((N_TOK, LANES), jnp.int32), jax.ShapeDtypeStruct((N_TOK, LANES), F32)],
        compiler_params=_cparams(("arbitrary",), 32),
        name="router")(h2, w_pad, b_pad)


def _tile_idx(i, nu_ref):
    return jnp.minimum(i, nu_ref[0] - 1)


def _moe_gu_kernel(te_ref, nu_ref, x_ref, wg_ref, wu_ref, bg_ref, bu_ref, h_ref, wg_s, wu_s):
    i = pl.program_id(1)

    @pl.when(i < nu_ref[0])
    def _():
        prev = te_ref[jnp.maximum(i - 1, 0)]

        @pl.when((i == 0) | (te_ref[i] != prev))
        def _():
            wg_s[...] = wg_ref[...].astype(BF16)
            wu_s[...] = wu_ref[...].astype(BF16)

        x = x_ref[...]
        gate = _dot(x, wg_s[...]) + bg_ref[...]
        up = _dot(x, wu_s[...]) + bu_ref[...]
        gate = jnp.minimum(gate, SWIGLU_LIMIT)
        up = jnp.clip(up, -SWIGLU_LIMIT, SWIGLU_LIMIT)
        glu = gate * _sigmoid(SWIGLU_ALPHA * gate)
        h_ref[...] = ((up + 1.0) * glu).astype(BF16)

    @pl.when(i >= nu_ref[0])
    def _():
        h_ref[...] = jnp.zeros(h_ref.shape, BF16)


def _moe_gu(xs, w_gu, b_gu3, tile_expert, n_used):
    nj = EXPERT_FF // MOE_TF
    grid_spec = pltpu.PrefetchScalarGridSpec(
        num_scalar_prefetch=2,
        grid=(nj, MOE_TILES),
        in_specs=[
            pl.BlockSpec((MOE_TM, D_MODEL), lambda j, i, te, nu: (_tile_idx(i, nu), 0)),
            pl.BlockSpec((None, D_MODEL, MOE_TF), lambda j, i, te, nu: (te[_tile_idx(i, nu)], 0, j)),
            pl.BlockSpec((None, D_MODEL, MOE_TF), lambda j, i, te, nu: (te[_tile_idx(i, nu)], 0, nj + j)),
            pl.BlockSpec((None, 1, MOE_TF), lambda j, i, te, nu: (te[_tile_idx(i, nu)], 0, j)),
            pl.BlockSpec((None, 1, MOE_TF), lambda j, i, te, nu: (te[_tile_idx(i, nu)], 0, nj + j)),
        ],
        out_specs=pl.BlockSpec((MOE_TM, MOE_TF), lambda j, i, te, nu: (i, j)),
        scratch_shapes=[pltpu.VMEM((D_MODEL, MOE_TF), BF16), pltpu.VMEM((D_MODEL, MOE_TF), BF16)])
    return pl.pallas_call(
        _moe_gu_kernel, grid_spec=grid_spec,
        out_shape=jax.ShapeDtypeStruct((MOE_SLOTS, EXPERT_FF), BF16),
        compiler_params=_cparams(("arbitrary", "arbitrary"), 52),
        name="moe_gate_up")(tile_expert, n_used, xs, w_gu, w_gu, b_gu3, b_gu3)


def _moe_down_kernel(te_ref, nu_ref, h_ref, w_ref, b_ref, sg_ref, y_ref, w_s):
    i = pl.program_id(1)

    @pl.when(i < nu_ref[0])
    def _():
        prev = te_ref[jnp.maximum(i - 1, 0)]

        @pl.when((i == 0) | (te_ref[i] != prev))
        def _():
            w_s[...] = w_ref[...].astype(BF16)

        y = _dot(h_ref[...], w_s[...]) + b_ref[...]
        y_ref[:, 0, :] = y * sg_ref[...]

    @pl.when(i >= nu_ref[0])
    def _():
        y_ref[...] = jnp.zeros(y_ref.shape, F32)


def _moe_down(hs, w_down, b_down3, slot_gate, tile_expert, n_used):
    nj = D_MODEL // MOE_TN
    grid_spec = pltpu.PrefetchScalarGridSpec(
        num_scalar_prefetch=2,
        grid=(nj, MOE_TILES),
        in_specs=[
            pl.BlockSpec((MOE_TM, EXPERT_FF), lambda j, i, te, nu: (_tile_idx(i, nu), 0)),
            pl.BlockSpec((None, EXPERT_FF, MOE_TN), lambda j, i, te, nu: (te[_tile_idx(i, nu)], 0, j)),
            pl.BlockSpec((None, 1, MOE_TN), lambda j, i, te, nu: (te[_tile_idx(i, nu)], 0, j)),
            pl.BlockSpec((MOE_TM, 1), lambda j, i, te, nu: (_tile_idx(i, nu), 0)),
        ],
        out_specs=pl.BlockSpec((MOE_TM, 1, MOE_TN), lambda j, i, te, nu: (i, 0, j)),
        scratch_shapes=[pltpu.VMEM((EXPERT_FF, MOE_TN), BF16)])
    return pl.pallas_call(
        _moe_down_kernel, grid_spec=grid_spec,
        out_shape=jax.ShapeDtypeStruct((MOE_SLOTS, 1, D_MODEL), F32),
        compiler_params=_cparams(("arbitrary", "arbitrary"), 40),
        name="moe_down")(tile_expert, n_used, hs, w_down, b_down3, slot_gate)


def _moe_plan(top_idx, gates):
    member = (top_idx[:, :, None] == jnp.arange(N_EXPERTS, dtype=jnp.int32)[None, None, :]).any(axis=1)
    member = member.astype(jnp.int32)
    before = jnp.cumsum(member, axis=0) - member
    counts = member.sum(axis=0)
    padded = (counts + MOE_TM - 1) // MOE_TM * MOE_TM
    pad_end = jnp.cumsum(padded)
    pad_start = pad_end - padded
    rank = jnp.take_along_axis(before, top_idx, axis=1)
    dest = (pad_start[top_idx] + rank).astype(jnp.int32)
    flat_dest = dest.reshape(-1)
    tok = jnp.repeat(jnp.arange(N_TOK, dtype=jnp.int32), TOP_K)
    slot_tok = jnp.zeros((MOE_SLOTS,), jnp.int32).at[flat_dest].set(tok, unique_indices=True)
    slot_gate = jnp.zeros((MOE_SLOTS,), F32).at[flat_dest].set(gates.reshape(-1), unique_indices=True)
    n_used = (pad_end[-1] // MOE_TM).astype(jnp.int32).reshape(1)
    tile_start = jnp.arange(MOE_TILES, dtype=jnp.int32) * MOE_TM
    tile_expert = jnp.minimum(jnp.searchsorted(pad_end, tile_start, side='right'), N_EXPERTS - 1).astype(jnp.int32)
    return slot_tok, slot_gate.reshape(MOE_SLOTS, 1), flat_dest, tile_expert, n_used


def _moe(h2, h2_rows, lw):
    idx_pad, gate_pad = _router(h2, lw['router_w'], lw['router_b'])
    top_idx = idx_pad[:, :TOP_K]
    gates = gate_pad[:, :TOP_K]
    slot_tok, slot_gate, dest, tile_expert, n_used = _moe_plan(top_idx, gates)
    xs = _gather_x(h2_rows, slot_tok)
    hs = _moe_gu(xs, lw['w_gu'], lw['b_gu'], tile_expert, n_used)
    ys = _moe_down(hs, lw['w_down'], lw['b_down'], slot_gate, tile_expert, n_used)
    return ys, dest


def _layer(x, cond16, lw, layer, caches, tabs):
    state_gla, state_ret, cache_gqa_k, cache_gqa_v, cache_nat_k, cache_nat_v = caches
    mod = _matmul(cond16, lw['mod_w'], lw['mod_b'], tm=16, tn=512, silu_a=True, vmem_mb=40, name="adaln_mod")
    mod3 = mod.reshape(16 * 6, 1, D_MODEL)

    h = _ln_mod(x, mod3)
    proj = _matmul(h, lw['w_in'], tabs['zero_bias'], tm=1024, tn=640, vmem_mb=48, name="in_proj")

    ya_c, st_a = _gla(proj, lw['gla_up'], lw['gla_ub'], lw['gla_norm_g'], latent=False)
    ya_l = _gla(proj, lw['gla_up'], lw['gla_ub'], lw['gla_norm_g'], latent=True, state=state_gla, layer=layer)[0]
    yb_c, st_b = _ret(proj, tabs['ret_lg'], lw['ret_gn_g'], lw['ret_gn_b'], latent=False)
    yb_l = _ret(proj, tabs['ret_lg'], lw['ret_gn_g'], lw['ret_gn_b'], latent=True,
                cos=tabs['cos64'], sin=tabs['sin64'], state=state_ret, layer=layer)[0]
    yc_c, kc = _ctx_attn(proj, COL_C_Q, COL_C_K, COL_C_V, GQA_KV_HEADS, GQA_HEADS // GQA_KV_HEADS,
                         lw['gqa_qn_g'], lw['gqa_kn_g'], norm=True, name="gqa_ctx")
    yc_l = _gqa_lat(proj, cache_gqa_k, cache_gqa_v, layer, lw['gqa_qn_g'], lw['gqa_kn_g'],
                    tabs['cos128'], tabs['sin128'])
    yd_c = _ctx_attn(proj, COL_D_Q, COL_D_K, COL_D_V, NAT_HEADS, 1, lw['gqa_qn_g'], lw['gqa_kn_g'],
                     norm=False, name="nat_ctx")[0]
    yd_l = _nat_lat(proj, cache_nat_k, cache_nat_v, layer, lw['nat_bias'])

    ys = [jnp.concatenate([c_, l_], axis=0) for c_, l_ in ((ya_c, ya_l), (yb_c, yb_l), (yc_c, yc_l), (yd_c, yd_l))]
    mix = _out_proj(ys, lw['w_o'])
    x1, h2, h2_rows = _res_ln1(x, mix, mod3, lw['ln1_g'], lw['ln1_b'])
    y_rows, dest = _moe(h2, h2_rows, lw)
    x2 = _res_ln2(x1, y_rows, dest, mod3, lw['ln2_g'], lw['ln2_b'])

    vc = proj[:N_PROMPT, COL_C_V:COL_C_V + GQA_KV_HEADS * GQA_DIM]
    kd = proj[:N_PROMPT, COL_D_K:COL_D_K + NAT_HEADS * NAT_DIM]
    vd = proj[:N_PROMPT, COL_D_V:COL_D_V + NAT_HEADS * NAT_DIM]
    new = (st_a, st_b,
           kc.reshape(BATCH, SEQ, GQA_KV_HEADS, GQA_DIM), vc.reshape(BATCH, SEQ, GQA_KV_HEADS, GQA_DIM),
           kd.reshape(BATCH, SEQ, NAT_HEADS, NAT_DIM), vd.reshape(BATCH, SEQ, NAT_HEADS, NAT_DIM))
    return x2, new


def _prep_layer_weights(l, mod_w, mod_b, w_in, gla_lr_up, gla_lr_b, gla_norm_g, ret_gn_g, ret_gn_b, gqa_qn_g,
                        gqa_kn_g, nat_rpb, w_o, ln1_g, ln1_b, router_w, router_b, w_gu, b_gu, w_down, b_down,
                        ln2_g, ln2_b):
    wi = w_in[l]
    w_in_p = jnp.concatenate([wi[:, :ALR_SRC], wi[:, ALR_SRC + ALR_W:], wi[:, ALR_SRC:ALR_SRC + ALR_W],
                              jnp.zeros((D_MODEL, PROJ_PAD - wi.shape[1]), wi.dtype)], axis=1).astype(BF16)
    up = jnp.zeros((2, LANES, GLA_HEADS * GLA_DK), F32)
    for d in range(2):
        up = up.at[d, d * GLA_RANK:(d + 1) * GLA_RANK, :].set(gla_lr_up[l, d])
    rw = jnp.zeros((D_MODEL, LANES), F32).at[:, :N_EXPERTS].set(router_w[l])
    rb = jnp.full((1, LANES), NEG_BIG, F32).at[0, :N_EXPERTS].set(router_b[l])
    return {
        'mod_w': mod_w[l], 'mod_b': mod_b[l].reshape(1, -1), 'w_in': w_in_p,
        'gla_up': up, 'gla_ub': gla_lr_b[l].reshape(2, 1, -1), 'gla_norm_g': gla_norm_g[l].reshape(1, -1),
        'ret_gn_g': ret_gn_g[l].reshape(1, -1), 'ret_gn_b': ret_gn_b[l].reshape(1, -1),
        'gqa_qn_g': gqa_qn_g[l].reshape(1, -1), 'gqa_kn_g': gqa_kn_g[l].reshape(1, -1),
        'nat_bias': _nat_bias_table(nat_rpb[l]), 'w_o': w_o[l],
        'ln1_g': ln1_g[l].reshape(1, -1), 'ln1_b': ln1_b[l].reshape(1, -1),
        'router_w': rw, 'router_b': rb,
        'w_gu': w_gu[l], 'b_gu': b_gu[l].reshape(N_EXPERTS, 1, -1),
        'w_down': w_down[l], 'b_down': b_down[l].reshape(N_EXPERTS, 1, -1),
        'ln2_g': ln2_g[l].reshape(1, -1), 'ln2_b': ln2_b[l].reshape(1, -1),
    }


def kernel(x_prompt, x_sample, state_gla, state_ret, cache_gqa_k, cache_gqa_v, cache_nat_k, cache_nat_v, c, c_ctx, mod_w, mod_b, w_in, gla_lr_up, gla_lr_b, gla_norm_g, ret_gn_g, ret_gn_b, gqa_qn_g, gqa_kn_g, nat_rpb, w_o, ln1_g, ln1_b, router_w, router_b, w_gu, b_gu, w_down, b_down, ln2_g, ln2_b):
    x = jnp.concatenate([x_prompt.reshape(N_PROMPT, D_MODEL), x_sample.reshape(N_SAMPLE, D_MODEL)], axis=0)
    cond16 = jnp.zeros((16, D_MODEL), F32).at[0].set(c_ctx).at[1:1 + DEC_BATCH].set(c)
    cos128, sin128 = _rope_tables(DEC_SEQ, GQA_DIM // 4)
    cos64, sin64 = _rope_tables(DEC_SEQ, RET_DK // 4)
    tabs = {'cos128': cos128, 'sin128': sin128, 'cos64': cos64, 'sin64': sin64,
            'ret_lg': jnp.asarray(_ret_log_decays(), F32), 'zero_bias': jnp.zeros((1, PROJ_PAD), F32)}
    caches = (state_gla.reshape(DEC_BATCH, DEPTH, 2, GLA_HEADS // 2, 2 * GLA_DK, GLA_DV),
              state_ret.reshape(DEC_BATCH, DEPTH, 2, RET_HEADS // 2, 2 * RET_DK, RET_DV),
              cache_gqa_k.reshape(DEC_BATCH, DEPTH, PAST_LEN, GQA_KV_HEADS * GQA_DIM),
              cache_gqa_v.reshape(DEC_BATCH, DEPTH, PAST_LEN, GQA_KV_HEADS * GQA_DIM),
              cache_nat_k.reshape(DEC_BATCH, DEPTH, PAST_LEN, NAT_HEADS * NAT_DIM),
              cache_nat_v.reshape(DEC_BATCH, DEPTH, PAST_LEN, NAT_HEADS * NAT_DIM))
    news = []
    for l in range(DEPTH):
        lw = _prep_layer_weights(l, mod_w, mod_b, w_in, gla_lr_up, gla_lr_b, gla_norm_g, ret_gn_g, ret_gn_b,
                                 gqa_qn_g, gqa_kn_g, nat_rpb, w_o, ln1_g, ln1_b, router_w, router_b, w_gu, b_gu,
                                 w_down, b_down, ln2_g, ln2_b)
        x, new = _layer(x, cond16, lw, l, caches, tabs)
        news.append(new)
    y_p = x[:N_PROMPT].reshape(BATCH, SEQ, D_MODEL)
    y_s = x[N_PROMPT:].reshape(DEC_BATCH, DEC_SEQ, D_MODEL)
    stacked = tuple(jnp.stack([news[l][k] for l in range(DEPTH)], axis=1) for k in range(6))
    return (y_p, y_s) + stacked
```

```python
import functools

import numpy as np
import jax
import jax.numpy as jnp
from jax import lax
from jax.experimental import pallas as pl
from jax.experimental.pallas import tpu as pltpu

F32 = jnp.float32
BF16 = jnp.bfloat16
U32 = jnp.uint32
I32 = jnp.int32

D_MODEL = 4096
BATCH = 16
SEQ = 256
DEPTH = 2
DEC_BATCH = 8
DEC_SEQ = 1024
PAST_LEN = 512
GRID_W = 64
GLA_HEADS = 8
GLA_DK = 64
GLA_DV = 128
GLA_RANK = 16
GLA_NORMALIZER = 16.0
RET_HEADS = 8
RET_DK = 64
RET_DV = 128
GQA_HEADS = 8
GQA_KV_HEADS = 2
GQA_DIM = 128
NAT_HEADS = 8
NAT_DIM = 128
NAT_ROWS = 8
NAT_COLS = 16
CHUNK = 64
ROPE_THETA = 10000.0
LN_EPS = 1e-6
N_EXPERTS = 32
TOP_K = 4
EXPERT_FF = 2048
SWIGLU_LIMIT = 7.0
SWIGLU_ALPHA = 1.702
DEEPNORM_ALPHA = (2.0 * DEPTH) ** 0.25

N_PROMPT = BATCH * SEQ
N_SAMPLE = DEC_BATCH * DEC_SEQ
N_TOK = N_PROMPT + N_SAMPLE

LANES = 128
SUBLANES = 8
ROW_TILE = 256
NEG_BIG = -1e30

COL_A_Q, COL_A_K, COL_A_V, COL_A_G = 0, 512, 1024, 2048
COL_B_Q, COL_B_K, COL_B_V, COL_B_G = 3072, 3584, 4096, 5120
COL_C_Q, COL_C_K, COL_C_V = 6144, 7168, 7424
COL_D_Q, COL_D_K, COL_D_V = 7680, 8704, 9728
COL_A_LR = 10752
PROJ_PAD = 10880
ALR_SRC = 3072
ALR_W = 2 * GLA_RANK

MOE_TM = 256
NK = N_TOK * TOP_K
MOE_TILES = NK // MOE_TM + N_EXPERTS
MOE_SLOTS = MOE_TILES * MOE_TM
MOE_TF = 512
MOE_TN = 2048
ROUTER_TM = 512

HALF = D_MODEL // 2
X_TILES = HALF // LANES
Y_PARTS = D_MODEL // MOE_TN
Y_TILES = MOE_TN // LANES
X_PITCH = X_TILES + SUBLANES
Y_PITCH = Y_TILES + SUBLANES


def _cparams(sem, vmem_mb):
    return pltpu.CompilerParams(dimension_semantics=sem, vmem_limit_bytes=int(vmem_mb * 1024 * 1024))


def _dot(a, b):
    return jnp.dot(a, b, preferred_element_type=F32)


def _dot_nt(a, b):
    return lax.dot_general(a, b, (((1,), (1,)), ((), ())), preferred_element_type=F32)


def _dot_tn(a, b):
    return lax.dot_general(a, b, (((0,), (0,)), ((), ())), preferred_element_type=F32)


def _sigmoid(x):
    return 1.0 / (1.0 + jnp.exp(-x))


def _silu(x):
    return x * _sigmoid(x)


def _ln_rows(x):
    mu = jnp.mean(x, axis=-1, keepdims=True)
    xc = x - mu
    var = jnp.mean(xc * xc, axis=-1, keepdims=True)
    return xc * lax.rsqrt(var + LN_EPS)


def _rms_rows(x):
    return x * lax.rsqrt(jnp.mean(x * x, axis=-1, keepdims=True) + LN_EPS)


def _pack_words(a, b):
    ab = pltpu.bitcast(a.astype(BF16).astype(F32), U32)
    bb = pltpu.bitcast(b.astype(BF16).astype(F32), U32)
    return (ab >> 16) | (bb & jnp.uint32(0xFFFF0000))


def _unpack_words(w):
    lo = pltpu.bitcast(w << 16, F32)
    hi = pltpu.bitcast(w & jnp.uint32(0xFFFF0000), F32)
    return lo, hi


def _mod_spec(which, tile):
    n_ctx_tiles = N_PROMPT // tile
    per_b = DEC_SEQ // tile

    def imap(i, *_):
        row = jnp.where(i < n_ctx_tiles, 0, 1 + (i - n_ctx_tiles) // per_b)
        return (row * 6 + which, 0, 0)
    return pl.BlockSpec((1, 1, D_MODEL), imap)


def _mm_kernel(a_ref, w_ref, b_ref, o_ref, *, silu_a):
    a = a_ref[...]
    if silu_a:
        a = _silu(a.astype(F32))
    acc = _dot(a.astype(BF16), w_ref[...].astype(BF16))
    o_ref[...] = acc + b_ref[...]


def _adaln_mod(cond16, mod_w, mod_b3, layer):
    k, n = mod_w.shape[1], mod_w.shape[2]
    tn = 512
    return pl.pallas_call(
        functools.partial(_mm_kernel, silu_a=True),
        grid=(1, n // tn),
        in_specs=[pl.BlockSpec((16, k), lambda i, j: (0, 0)),
                  pl.BlockSpec((None, k, tn), lambda i, j: (layer, 0, j)),
                  pl.BlockSpec((None, 1, tn), lambda i, j: (layer, 0, j))],
        out_specs=pl.BlockSpec((16, tn), lambda i, j: (0, j)),
        out_shape=jax.ShapeDtypeStruct((16, n), F32),
        compiler_params=_cparams(("arbitrary", "arbitrary"), 40),
        name="adaln_mod")(cond16, mod_w, mod_b3)


def _in_proj(h, w_in_p, zero_bias):
    tm, tn = 1024, 640
    k = h.shape[1]
    return pl.pallas_call(
        functools.partial(_mm_kernel, silu_a=False),
        grid=(N_TOK // tm, PROJ_PAD // tn),
        in_specs=[pl.BlockSpec((tm, k), lambda i, j: (i, 0)),
                  pl.BlockSpec((k, tn), lambda i, j: (0, j)),
                  pl.BlockSpec((1, tn), lambda i, j: (0, j))],
        out_specs=pl.BlockSpec((tm, tn), lambda i, j: (i, j)),
        out_shape=jax.ShapeDtypeStruct((N_TOK, PROJ_PAD), F32),
        compiler_params=_cparams(("arbitrary", "arbitrary"), 48),
        name="in_proj")(h, w_in_p, zero_bias)


def _mm4_kernel(a0, a1, a2, a3, w0, w1, w2, w3, o_ref):
    acc = _dot(a0[...], w0[...].astype(BF16))
    acc += _dot(a1[...], w1[...].astype(BF16))
    acc += _dot(a2[...], w2[...].astype(BF16))
    acc += _dot(a3[...], w3[...].astype(BF16))
    o_ref[...] = acc


def _out_proj(ys, w_o, layer, *, tm=1024, tn=512):
    kp = ys[0].shape[1]
    a_specs = [pl.BlockSpec((tm, kp), lambda i, j: (i, 0)) for _ in range(4)]
    w_specs = [pl.BlockSpec((None, kp, tn), functools.partial(lambda i, j, p: (layer, p, j), p=p)) for p in range(4)]
    return pl.pallas_call(
        _mm4_kernel,
        grid=(N_TOK // tm, D_MODEL // tn),
        in_specs=a_specs + w_specs,
        out_specs=pl.BlockSpec((tm, tn), lambda i, j: (i, j)),
        out_shape=jax.ShapeDtypeStruct((N_TOK, D_MODEL), F32),
        compiler_params=_cparams(("arbitrary", "arbitrary"), 48),
        name="out_proj")(*ys, w_o, w_o, w_o, w_o)


def _ln_mod_kernel(x_ref, sh_ref, sc_ref, h_ref):
    h = _ln_rows(x_ref[...]) * (1.0 + sc_ref[0]) + sh_ref[0]
    h_ref[...] = h.astype(BF16)


def _ln_mod(x, mod3):
    return pl.pallas_call(
        _ln_mod_kernel,
        grid=(N_TOK // ROW_TILE,),
        in_specs=[pl.BlockSpec((ROW_TILE, D_MODEL), lambda i: (i, 0)), _mod_spec(0, ROW_TILE),
                  _mod_spec(1, ROW_TILE)],
        out_specs=pl.BlockSpec((ROW_TILE, D_MODEL), lambda i: (i, 0)),
        out_shape=jax.ShapeDtypeStruct((N_TOK, D_MODEL), BF16),
        compiler_params=_cparams(("arbitrary",), 32),
        name="ln_mod")(x, mod3, mod3)


def _res_ln1_kernel(x_ref, mix_ref, gate_ref, g_ref, b_ref, sh_ref, sc_ref, x1_ref, h2_ref, rows_ref):
    y = DEEPNORM_ALPHA * x_ref[...] + gate_ref[0] * mix_ref[...]
    x1 = _ln_rows(y) * g_ref[...] + b_ref[...]
    x1_ref[...] = x1
    h2 = _ln_rows(x1) * (1.0 + sc_ref[0]) + sh_ref[0]
    h2_ref[...] = h2.astype(BF16)
    for c in range(X_TILES):
        lo = h2[:, c * LANES:(c + 1) * LANES]
        hi = h2[:, HALF + c * LANES:HALF + (c + 1) * LANES]
        rows_ref[pl.ds(c, ROW_TILE, stride=X_TILES), :] = _pack_words(lo, hi)


def _res_ln1(x, mix, mod3, ln_g, ln_b):
    tile = ROW_TILE
    row_spec = pl.BlockSpec((tile, D_MODEL), lambda i: (i, 0))
    vec_spec = pl.BlockSpec((1, D_MODEL), lambda i: (0, 0))
    x1, h2, rows = pl.pallas_call(
        _res_ln1_kernel,
        grid=(N_TOK // tile,),
        in_specs=[row_spec, row_spec, _mod_spec(2, tile), vec_spec, vec_spec, _mod_spec(3, tile), _mod_spec(4, tile)],
        out_specs=[row_spec, row_spec, pl.BlockSpec((tile * X_TILES, LANES), lambda i: (i, 0))],
        out_shape=[jax.ShapeDtypeStruct((N_TOK, D_MODEL), F32), jax.ShapeDtypeStruct((N_TOK, D_MODEL), BF16),
                   jax.ShapeDtypeStruct((N_TOK * X_TILES, LANES), U32)],
        compiler_params=_cparams(("arbitrary",), 48),
        name="res_ln_mixer")(x, mix, mod3, ln_g, ln_b, mod3, mod3)
    return x1, h2, rows.reshape(1, N_TOK, X_TILES, LANES)


def _gather_step(idx_cur, idx_nxt, src_ref, buf, sems, n_rows, tiles, pitch):
    step = pl.program_id(0)
    slot = step % 2

    def issue(idx_ref, to_slot):
        def body(r, carry):
            dst = buf.at[to_slot, :, pl.ds(pl.multiple_of(r * pitch, SUBLANES), tiles), :]
            pltpu.make_async_copy(src_ref.at[:, idx_ref[0, 0, r]], dst, sems.at[to_slot]).start()
            return carry
        lax.fori_loop(0, n_rows, body, 0, unroll=8)

    @pl.when(step == 0)
    def _():
        issue(idx_cur, slot)

    @pl.when(step + 1 < pl.num_programs(0))
    def _():
        issue(idx_nxt, 1 - slot)

    landed = buf.at[slot, :, pl.ds(0, n_rows * tiles), :]
    pltpu.make_async_copy(landed, landed, sems.at[slot]).wait()
    return slot


def _idx_specs(n_rows, n_steps):
    return [pl.BlockSpec((1, 1, n_rows), lambda i: (i, 0, 0), memory_space=pltpu.SMEM),
            pl.BlockSpec((1, 1, n_rows), lambda i: (jnp.minimum(i + 1, n_steps - 1), 0, 0),
                         memory_space=pltpu.SMEM)]


def _res_ln2_kernel(idx_cur, idx_nxt, x_ref, y_hbm, gates_ref, gate_ref, g_ref, b_ref, o_ref, buf, sems, mix_scr,
                    *, tile):
    slot = _gather_step(idx_cur, idx_nxt, y_hbm, buf, sems, TOP_K * tile, Y_TILES, Y_PITCH)
    gates = gates_ref[...]
    for part in range(Y_PARTS):
        for c in range(Y_TILES):
            acc = None
            for kk in range(TOP_K):
                slab = buf[slot, part, pl.ds(kk * tile * Y_PITCH + c, tile, stride=Y_PITCH), :]
                term = gates[:, kk:kk + 1] * slab
                acc = term if acc is None else acc + term
            col = part * MOE_TN + c * LANES
            mix_scr[:, col:col + LANES] = acc
    y = DEEPNORM_ALPHA * x_ref[...] + gate_ref[0] * mix_scr[...]
    o_ref[...] = _ln_rows(y) * g_ref[...] + b_ref[...]


def _res_ln2(x1, y_rows, dest, gates_pad, mod3, ln_g, ln_b):
    tile = 64
    n_tiles = N_TOK // tile
    idx = dest.reshape(n_tiles, tile, TOP_K).transpose(0, 2, 1).reshape(n_tiles, 1, TOP_K * tile)
    row_spec = pl.BlockSpec((tile, D_MODEL), lambda i: (i, 0))
    vec_spec = pl.BlockSpec((1, D_MODEL), lambda i: (0, 0))
    return pl.pallas_call(
        functools.partial(_res_ln2_kernel, tile=tile),
        grid=(n_tiles,),
        in_specs=_idx_specs(TOP_K * tile, n_tiles) + [row_spec, pl.BlockSpec(memory_space=pl.ANY),
                                                      pl.BlockSpec((tile, LANES), lambda i: (i, 0)),
                                                      _mod_spec(5, tile), vec_spec, vec_spec],
        out_specs=row_spec,
        out_shape=jax.ShapeDtypeStruct((N_TOK, D_MODEL), F32),
        scratch_shapes=[pltpu.VMEM((2, Y_PARTS, TOP_K * tile * Y_PITCH, LANES), F32), pltpu.SemaphoreType.DMA((2,)),
                        pltpu.VMEM((tile, D_MODEL), F32)],
        compiler_params=_cparams(("arbitrary",), 40),
        name="res_ln_moe")(idx, idx, x1, y_rows, gates_pad, mod3, ln_g, ln_b)


def _gather_x_kernel(idx_cur, idx_nxt, src_hbm, o_ref, buf, sems):
    slot = _gather_step(idx_cur, idx_nxt, src_hbm, buf, sems, MOE_TM, X_TILES, X_PITCH)
    for c in range(X_TILES):
        lo, hi = _unpack_words(buf[slot, 0, pl.ds(c, MOE_TM, stride=X_PITCH), :])
        o_ref[:, c * LANES:(c + 1) * LANES] = lo.astype(BF16)
        o_ref[:, HALF + c * LANES:HALF + (c + 1) * LANES] = hi.astype(BF16)


def _gather_x(h2_rows, slot_tok):
    idx = slot_tok.reshape(MOE_TILES, 1, MOE_TM)
    return pl.pallas_call(
        _gather_x_kernel,
        grid=(MOE_TILES,),
        in_specs=_idx_specs(MOE_TM, MOE_TILES) + [pl.BlockSpec(memory_space=pl.ANY)],
        out_specs=pl.BlockSpec((MOE_TM, D_MODEL), lambda i: (i, 0)),
        out_shape=jax.ShapeDtypeStruct((MOE_SLOTS, D_MODEL), BF16),
        scratch_shapes=[pltpu.VMEM((2, 1, MOE_TM * X_PITCH, LANES), U32), pltpu.SemaphoreType.DMA((2,))],
        compiler_params=_cparams(("arbitrary",), 32),
        name="moe_gather_x")(idx, idx, h2_rows)


def _rope(x, cos, sin_signed, quarter):
    lane = lax.broadcasted_iota(jnp.int32, x.shape, 1)
    up = pltpu.roll(x, LANES - quarter, axis=1)
    dn = pltpu.roll(x, quarter, axis=1)
    partner = jnp.where((lane % (2 * quarter)) < quarter, up, dn)
    return x * cos + partner * sin_signed


def _rope_tables(t_len, quarter):
    lane = np.arange(LANES)
    within = lane % (4 * quarter)
    is_col = within >= 2 * quarter
    j = within % quarter
    second = (within % (2 * quarter)) >= quarter
    freqs = ROPE_THETA ** (-(j.astype(np.float32)) / quarter)
    t = np.arange(t_len)
    pos = np.where(is_col[None, :], (t % GRID_W)[:, None], (t // GRID_W)[:, None]).astype(np.float32)
    ang = pos * freqs[None, :].astype(np.float32)
    cos = np.cos(ang).astype(np.float32)
    sin = np.sin(ang).astype(np.float32) * np.where(second, 1.0, -1.0)[None, :].astype(np.float32)
    return jnp.asarray(cos), jnp.asarray(sin)


def _attend(q, keys, vals, scale, biases=None):
    ss = []
    for idx, k in enumerate(keys):
        s = _dot_nt(q, k) * scale
        if biases is not None and biases[idx] is not None:
            s = s + biases[idx]
        ss.append(s)
    m = ss[0].max(axis=-1, keepdims=True)
    for s in ss[1:]:
        m = jnp.maximum(m, s.max(axis=-1, keepdims=True))
    l = None
    o = None
    for s, v in zip(ss, vals):
        p = jnp.exp(s - m)
        pl_sum = p.sum(axis=-1, keepdims=True)
        po = _dot(p.astype(BF16), v)
        l = pl_sum if l is None else l + pl_sum
        o = po if o is None else o + po
    return o / l


def _ctx_attn_kernel(q_ref, k_ref, v_ref, gq_ref, gk_ref, o_ref, *rest, groups, norm):
    k = k_ref[...]
    if norm:
        k = _rms_rows(k) * gk_ref[...]
        rest[0][...] = k
    kb = k.astype(BF16)
    vb = v_ref[...].astype(BF16)
    scale = GQA_DIM ** -0.5
    for g in range(groups):
        q = q_ref[:, g * LANES:(g + 1) * LANES]
        if norm:
            q = _rms_rows(q) * gq_ref[...]
        o_ref[:, g * LANES:(g + 1) * LANES] = _attend(q.astype(BF16), [kb], [vb], scale).astype(BF16)


def _ctx_attn(proj, col_q, col_k, col_v, n_kv, groups, gq, gk, *, norm, name):
    qw = groups * LANES
    in_specs = [pl.BlockSpec((SEQ, qw), lambda b, h: (b, col_q // qw + h)),
                pl.BlockSpec((SEQ, LANES), lambda b, h: (b, col_k // LANES + h)),
                pl.BlockSpec((SEQ, LANES), lambda b, h: (b, col_v // LANES + h)),
                pl.BlockSpec((1, LANES), lambda b, h: (0, 0)),
                pl.BlockSpec((1, LANES), lambda b, h: (0, 0))]
    out_specs = [pl.BlockSpec((SEQ, qw), lambda b, h: (b, h))]
    out_shape = [jax.ShapeDtypeStruct((N_PROMPT, n_kv * qw), BF16)]
    if norm:
        out_specs.append(pl.BlockSpec((SEQ, LANES), lambda b, h: (b, h)))
        out_shape.append(jax.ShapeDtypeStruct((N_PROMPT, n_kv * LANES), F32))
    return pl.pallas_call(
        functools.partial(_ctx_attn_kernel, groups=groups, norm=norm),
        grid=(BATCH, n_kv),
        in_specs=in_specs, out_specs=out_specs, out_shape=out_shape,
        compiler_params=_cparams(("arbitrary", "arbitrary"), 32),
        name=name)(proj, proj, proj, gq, gk)


def _gqa_lat_kernel(q_ref, k_ref, v_ref, ck_ref, cv_ref, gq_ref, gk_ref, cosq_ref, sinq_ref, cosk_ref, sink_ref,
                    o_ref, k_scr, v_scr):
    qb = pl.program_id(2)

    @pl.when(qb == 0)
    def _():
        k = _rms_rows(k_ref[...]) * gk_ref[...]
        k_scr[...] = _rope(k, cosk_ref[...], sink_ref[...], GQA_DIM // 4).astype(BF16)
        v_scr[...] = v_ref[...].astype(BF16)

    ck = ck_ref[...].astype(BF16)
    cv = cv_ref[...].astype(BF16)
    scale = GQA_DIM ** -0.5
    for g in range(GQA_HEADS // GQA_KV_HEADS):
        q = _rms_rows(q_ref[:, g * LANES:(g + 1) * LANES]) * gq_ref[...]
        q = _rope(q, cosq_ref[...], sinq_ref[...], GQA_DIM // 4).astype(BF16)
        o = _attend(q, [k_scr[...], ck], [v_scr[...], cv], scale)
        o_ref[:, g * LANES:(g + 1) * LANES] = o.astype(BF16)


def _gqa_lat(proj, cache_k, cache_v, layer, gq, gk, cos, sin):
    groups = GQA_HEADS // GQA_KV_HEADS
    qw = groups * LANES
    tq = ROW_TILE
    nqb = DEC_SEQ // tq
    row0 = N_PROMPT // tq
    in_specs = [
        pl.BlockSpec((tq, qw), lambda b, h, i: (row0 + b * nqb + i, COL_C_Q // qw + h)),
        pl.BlockSpec((DEC_SEQ, LANES), lambda b, h, i: (N_PROMPT // DEC_SEQ + b, COL_C_K // LANES + h)),
        pl.BlockSpec((DEC_SEQ, LANES), lambda b, h, i: (N_PROMPT // DEC_SEQ + b, COL_C_V // LANES + h)),
        pl.BlockSpec((None, None, PAST_LEN, LANES), lambda b, h, i: (b, layer, 0, h)),
        pl.BlockSpec((None, None, PAST_LEN, LANES), lambda b, h, i: (b, layer, 0, h)),
        pl.BlockSpec((1, LANES), lambda b, h, i: (0, 0)),
        pl.BlockSpec((1, LANES), lambda b, h, i: (0, 0)),
        pl.BlockSpec((tq, LANES), lambda b, h, i: (i, 0)),
        pl.BlockSpec((tq, LANES), lambda b, h, i: (i, 0)),
        pl.BlockSpec((DEC_SEQ, LANES), lambda b, h, i: (0, 0)),
        pl.BlockSpec((DEC_SEQ, LANES), lambda b, h, i: (0, 0)),
    ]
    return pl.pallas_call(
        _gqa_lat_kernel,
        grid=(DEC_BATCH, GQA_KV_HEADS, nqb),
        in_specs=in_specs,
        out_specs=pl.BlockSpec((tq, qw), lambda b, h, i: (b * nqb + i, h)),
        out_shape=jax.ShapeDtypeStruct((N_SAMPLE, GQA_HEADS * GQA_DIM), BF16),
        scratch_shapes=[pltpu.VMEM((DEC_SEQ, LANES), BF16), pltpu.VMEM((DEC_SEQ, LANES), BF16)],
        compiler_params=_cparams(("arbitrary", "arbitrary", "arbitrary"), 40),
        name="gqa_latent")(proj, proj, proj, cache_k, cache_v, gq, gk, cos, sin, cos, sin)


NAT_GRID_ROWS = DEC_SEQ // GRID_W
NAT_KR = min(NAT_ROWS, NAT_GRID_ROWS)
NAT_BAND = NAT_KR * GRID_W
NAT_PATTERNS = NAT_KR


def _nat_lat_kernel(q_ref, k_ref, v_ref, ck_ref, cv_ref, bias_ref, o_ref, k_scr, v_scr, ck_scr, cv_scr):
    k_scr[...] = k_ref[...].astype(BF16)
    v_scr[...] = v_ref[...].astype(BF16)
    ck_scr[...] = ck_ref[...].astype(BF16)
    cv_scr[...] = cv_ref[...].astype(BF16)
    scale = NAT_DIM ** -0.5

    def body(r, carry):
        band = jnp.clip(r - NAT_KR // 2, 0, NAT_GRID_ROWS - NAT_KR)
        start = pl.multiple_of(band * GRID_W, GRID_W)
        rows = pl.ds(pl.multiple_of(r * GRID_W, GRID_W), GRID_W)
        q = q_ref[rows, :].astype(BF16)
        kb = k_scr[pl.ds(start, NAT_BAND), :]
        vb = v_scr[pl.ds(start, NAT_BAND), :]
        bias = bias_ref[r - band]
        o = _attend(q, [kb, ck_scr[...]], [vb, cv_scr[...]], scale, biases=[bias, None])
        o_ref[rows, :] = o.astype(BF16)
        return carry

    lax.fori_loop(0, NAT_GRID_ROWS, body, 0)


def _nat_lat(proj, cache_k, cache_v, layer, bias_tab):
    rb = N_PROMPT // DEC_SEQ
    in_specs = [
        pl.BlockSpec((DEC_SEQ, LANES), lambda b, h: (rb + b, COL_D_Q // LANES + h)),
        pl.BlockSpec((DEC_SEQ, LANES), lambda b, h: (rb + b, COL_D_K // LANES + h)),
        pl.BlockSpec((DEC_SEQ, LANES), lambda b, h: (rb + b, COL_D_V // LANES + h)),
        pl.BlockSpec((None, None, PAST_LEN, LANES), lambda b, h: (b, layer, 0, h)),
        pl.BlockSpec((None, None, PAST_LEN, LANES), lambda b, h: (b, layer, 0, h)),
        pl.BlockSpec((None, NAT_PATTERNS, GRID_W, NAT_BAND), lambda b, h: (h, 0, 0, 0)),
    ]
    return pl.pallas_call(
        _nat_lat_kernel,
        grid=(DEC_BATCH, NAT_HEADS),
        in_specs=in_specs,
        out_specs=pl.BlockSpec((DEC_SEQ, LANES), lambda b, h: (b, h)),
        out_shape=jax.ShapeDtypeStruct((N_SAMPLE, NAT_HEADS * NAT_DIM), BF16),
        scratch_shapes=[pltpu.VMEM((DEC_SEQ, LANES), BF16), pltpu.VMEM((DEC_SEQ, LANES), BF16),
                        pltpu.VMEM((PAST_LEN, LANES), BF16), pltpu.VMEM((PAST_LEN, LANES), BF16)],
        compiler_params=_cparams(("arbitrary", "arbitrary"), 32),
        name="nat_latent")(proj, proj, proj, cache_k, cache_v, bias_tab)


def _nat_bias_table(rpb):
    off = np.arange(NAT_PATTERNS)
    kr = np.arange(NAT_KR)
    rel_r = kr[None, :] - off[:, None] + NAT_ROWS - 1
    c = np.arange(GRID_W)
    rel_c = np.clip(c[None, :] - c[:, None], 1 - NAT_COLS, NAT_COLS - 1) + NAT_COLS - 1
    col_start = np.clip(c - NAT_COLS // 2, 0, GRID_W - NAT_COLS)
    col_ok = (c[None, :] >= col_start[:, None]) & (c[None, :] < col_start[:, None] + NAT_COLS)
    bias = rpb[:, rel_r[:, None, :, None], rel_c[None, :, None, :]]
    bias = jnp.where(jnp.asarray(col_ok)[None, None, :, None, :], bias.astype(F32), NEG_BIG)
    return bias.reshape(NAT_HEADS, NAT_PATTERNS, GRID_W, NAT_BAND)


def _ret_log_decays():
    expo = -5.0 - np.arange(RET_HEADS, dtype=np.float64)
    return np.log1p(-np.exp2(expo))


def _ret_kernel(lg_ref, q_ref, k_ref, v_ref, g_ref, gng_ref, gnb_ref, *rest, t_len, latent):
    if latent:
        cos_ref, sin_ref, s0_ref, y_ref = rest
    else:
        y_ref, st_ref = rest
    p = pl.program_id(1)
    q = q_ref[...]
    k = k_ref[...] * (RET_DK ** -0.5)
    if latent:
        q = _rope(q, cos_ref[...], sin_ref[...], RET_DK // 4)
        k = _rope(k, cos_ref[...], sin_ref[...], RET_DK // 4)
    kb = k.astype(BF16)
    lane = lax.broadcasted_iota(jnp.int32, (t_len, LANES), 1)
    tq = ROW_TILE
    for h in range(2):
        lgf = lg_ref[2 * p + h]
        lgb = lg_ref[RET_HEADS - 1 - (2 * p + h)]
        head_lanes = (lane >= h * RET_DK) & (lane < (h + 1) * RET_DK)
        qh = jnp.where(head_lanes, q, 0.0).astype(BF16)
        vh = v_ref[:, h * RET_DV:(h + 1) * RET_DV].astype(BF16)
        for blk in range(t_len // tq):
            r0 = blk * tq
            i = lax.broadcasted_iota(jnp.int32, (tq, t_len), 0) + r0
            j = lax.broadcasted_iota(jnp.int32, (tq, t_len), 1)
            diff = (i - j).astype(F32)
            arg = jnp.where(diff > 0, lgf * diff, -lgb * diff)
            dmat = jnp.exp(arg) * jnp.where(diff == 0, 2.0, 1.0)
            qblk = qh[r0:r0 + tq]
            s = _dot_nt(qblk, kb) * dmat
            o = _dot(s.astype(BF16), vh)
            if latent:
                pos = (lax.broadcasted_iota(jnp.int32, (tq, 1), 0) + r0).astype(F32)
                o = o + _dot(qblk, s0_ref[0].astype(BF16)) * jnp.exp(lgf * (pos + 1.0))
                o = o + _dot(qblk, s0_ref[1].astype(BF16)) * jnp.exp(lgb * (t_len - pos))
            y = _ln_rows(o) * gng_ref[:, h * RET_DV:(h + 1) * RET_DV] + gnb_ref[:, h * RET_DV:(h + 1) * RET_DV]
            gate = g_ref[r0:r0 + tq, h * RET_DV:(h + 1) * RET_DV]
            y_ref[r0:r0 + tq, h * RET_DV:(h + 1) * RET_DV] = (y * _silu(gate)).astype(BF16)
        if not latent:
            pos = lax.broadcasted_iota(jnp.int32, (t_len, 1), 0).astype(F32)
            kf = (k * jnp.exp(lgf * (t_len - 1.0 - pos))).astype(BF16)
            kbw = (k * jnp.exp(lgb * pos)).astype(BF16)
            st_ref[0, h] = _dot_tn(kf, vh)[h * RET_DK:(h + 1) * RET_DK, :]
            st_ref[1, h] = _dot_tn(kbw, vh)[h * RET_DK:(h + 1) * RET_DK, :]


def _ret(proj, lg, gn_g, gn_b, *, latent, cos=None, sin=None, state=None, layer=None):
    t_len = DEC_SEQ if latent else SEQ
    nb = DEC_BATCH if latent else BATCH
    rb = N_PROMPT // DEC_SEQ if latent else 0
    pw = 2 * RET_DV
    in_specs = [
        pl.BlockSpec(memory_space=pltpu.SMEM),
        pl.BlockSpec((t_len, LANES), lambda b, p: (rb + b, COL_B_Q // LANES + p)),
        pl.BlockSpec((t_len, LANES), lambda b, p: (rb + b, COL_B_K // LANES + p)),
        pl.BlockSpec((t_len, pw), lambda b, p: (rb + b, COL_B_V // pw + p)),
        pl.BlockSpec((t_len, pw), lambda b, p: (rb + b, COL_B_G // pw + p)),
        pl.BlockSpec((1, pw), lambda b, p: (0, p)),
        pl.BlockSpec((1, pw), lambda b, p: (0, p)),
    ]
    args = [lg, proj, proj, proj, proj, gn_g, gn_b]
    out_specs = [pl.BlockSpec((t_len, pw), lambda b, p: (b, p))]
    out_shape = [jax.ShapeDtypeStruct((nb * t_len, RET_HEADS * RET_DV), BF16)]
    if latent:
        in_specs += [pl.BlockSpec((t_len, LANES), lambda b, p: (0, 0)),
                     pl.BlockSpec((t_len, LANES), lambda b, p: (0, 0)),
                     pl.BlockSpec((None, None, 2, None, 2 * RET_DK, RET_DV), lambda b, p: (b, layer, 0, p, 0, 0))]
        args += [cos, sin, state]
    else:
        out_specs.append(pl.BlockSpec((None, 2, 2, RET_DK, RET_DV), lambda b, p: (b, 0, p, 0, 0)))
        out_shape.append(jax.ShapeDtypeStruct((nb, 2, RET_HEADS, RET_DK, RET_DV), F32))
    return pl.pallas_call(
        functools.partial(_ret_kernel, t_len=t_len, latent=latent),
        grid=(nb, RET_HEADS // 2),
        in_specs=in_specs, out_specs=out_specs, out_shape=out_shape,
        compiler_params=_cparams(("arbitrary", "arbitrary"), 48),
        name="ret_latent" if latent else "ret_ctx")(*args)


def _gla_kernel(q_ref, k_ref, v_ref, g_ref, lr_ref, up_ref, ub_ref, ng_ref, *rest, t_len, latent):
    if latent:
        s0_ref, y_ref, la_scr, o_scr, st_scr = rest
    else:
        y_ref, stout_ref, la_scr, o_scr, st_scr = rest
    n_chunks = t_len // CHUNK
    lr = lr_ref[...].astype(BF16)
    for d in range(2):
        z = _dot(lr, up_ref[d].astype(BF16)) + ub_ref[d]
        la_scr[d] = (jnp.minimum(z, 0.0) - jnp.log(1.0 + jnp.exp(-jnp.abs(z)))) / GLA_NORMALIZER

    ci = lax.broadcasted_iota(jnp.int32, (CHUNK, CHUNK), 0)
    cj = lax.broadcasted_iota(jnp.int32, (CHUNK, CHUNK), 1)
    lane = lax.broadcasted_iota(jnp.int32, (CHUNK, LANES), 1)
    head_lanes = [(lane >= h * GLA_DK) & (lane < (h + 1) * GLA_DK) for h in range(2)]

    for d in range(2):
        keep = (cj <= ci) if d == 0 else (cj >= ci)
        tri = jnp.where(keep, 1.0, 0.0).astype(BF16)
        if latent:
            s0t = s0_ref[d].T
            st_scr[0] = s0t
            st_scr[1] = s0t
        else:
            st_scr[0] = jnp.zeros((GLA_DV, LANES), F32)
            st_scr[1] = jnp.zeros((GLA_DV, LANES), F32)

        def body(step, carry, d=d, keep=keep, tri=tri):
            c = step if d == 0 else n_chunks - 1 - step
            rows = pl.ds(pl.multiple_of(c * CHUNK, CHUNK), CHUNK)
            la = la_scr[d, rows, :]
            la_hi = la.astype(BF16)
            la_lo = (la - la_hi.astype(F32)).astype(BF16)
            bcum = _dot(tri, la_hi) + _dot(tri, la_lo)
            btot = bcum[CHUNK - 1:CHUNK, :] if d == 0 else bcum[0:1, :]
            q_in = q_ref[rows, :] * (GLA_DK ** -0.5) * jnp.exp(bcum)
            kk = k_ref[rows, :]
            k_in = (kk * jnp.exp(-bcum)).astype(BF16)
            k_out = (kk * jnp.exp(btot - bcum)).astype(BF16)
            dec = jnp.exp(btot)
            for h in range(2):
                qm = jnp.where(head_lanes[h], q_in, 0.0).astype(BF16)
                vh = v_ref[rows, h * GLA_DV:(h + 1) * GLA_DV].astype(BF16)
                sc = jnp.where(keep, _dot_nt(qm, k_in), 0.0).astype(BF16)
                st = st_scr[h]
                o = _dot(sc, vh) + _dot_nt(qm, st.astype(BF16))
                st_scr[h] = st * dec + _dot_tn(vh, k_out)
                if d == 0:
                    o_scr[rows, h * GLA_DV:(h + 1) * GLA_DV] = o
                else:
                    o_scr[rows, h * GLA_DV:(h + 1) * GLA_DV] += o
            return carry

        lax.fori_loop(0, n_chunks, body, 0)
        if not latent:
            for h in range(2):
                stout_ref[d, h] = st_scr[h].T[h * GLA_DK:(h + 1) * GLA_DK, :]

    for h in range(2):
        cols = slice(h * GLA_DV, (h + 1) * GLA_DV)
        y = _rms_rows(o_scr[:, cols]) * ng_ref[...]
        y_ref[:, cols] = (y.astype(F32) * _silu(g_ref[:, cols])).astype(BF16)


def _gla(proj, up_mat, up_bias, norm_g, *, latent, state=None, layer=None):
    t_len = DEC_SEQ if latent else SEQ
    nb = DEC_BATCH if latent else BATCH
    rb = N_PROMPT // DEC_SEQ if latent else 0
    pw = 2 * GLA_DV
    in_specs = [
        pl.BlockSpec((t_len, LANES), lambda b, p: (rb + b, COL_A_Q // LANES + p)),
        pl.BlockSpec((t_len, LANES), lambda b, p: (rb + b, COL_A_K // LANES + p)),
        pl.BlockSpec((t_len, pw), lambda b, p: (rb + b, COL_A_V // pw + p)),
        pl.BlockSpec((t_len, pw), lambda b, p: (rb + b, COL_A_G // pw + p)),
        pl.BlockSpec((t_len, LANES), lambda b, p: (rb + b, COL_A_LR // LANES)),
        pl.BlockSpec((2, LANES, LANES), lambda b, p: (0, 0, p)),
        pl.BlockSpec((2, 1, LANES), lambda b, p: (0, 0, p)),
        pl.BlockSpec((1, GLA_DV), lambda b, p: (0, 0)),
    ]
    args = [proj, proj, proj, proj, proj, up_mat, up_bias, norm_g]
    out_specs = [pl.BlockSpec((t_len, pw), lambda b, p: (b, p))]
    out_shape = [jax.ShapeDtypeStruct((nb * t_len, GLA_HEADS * GLA_DV), BF16)]
    if latent:
        in_specs.append(pl.BlockSpec((None, None, 2, None, 2 * GLA_DK, GLA_DV),
                                     lambda b, p: (b, layer, 0, p, 0, 0)))
        args.append(state)
    else:
        out_specs.append(pl.BlockSpec((None, 2, 2, GLA_DK, GLA_DV), lambda b, p: (b, 0, p, 0, 0)))
        out_shape.append(jax.ShapeDtypeStruct((nb, 2, GLA_HEADS, GLA_DK, GLA_DV), F32))
    scratch = [pltpu.VMEM((2, t_len, LANES), F32), pltpu.VMEM((t_len, pw), F32),
               pltpu.VMEM((2, GLA_DV, LANES), F32)]
    return pl.pallas_call(
        functools.partial(_gla_kernel, t_len=t_len, latent=latent),
        grid=(nb, GLA_HEADS // 2),
        in_specs=in_specs, out_specs=out_specs, out_shape=out_shape, scratch_shapes=scratch,
        compiler_params=_cparams(("arbitrary", "arbitrary"), 32),
        name="gla_latent" if latent else "gla_ctx")(*args)


def _router_kernel(h_ref, w_ref, b_ref, idx_ref, gate_ref, rank_ref, cnt_ref, run_scr):
    step = pl.program_id(0)

    @pl.when(step == 0)
    def _():
        run_scr[...] = jnp.zeros(run_scr.shape, F32)

    logits = _dot(h_ref[...], w_ref[...].astype(BF16)) + b_ref[...]
    lane = lax.broadcasted_iota(jnp.int32, logits.shape, 1)
    idx_out = jnp.zeros(logits.shape, jnp.int32)
    val_out = jnp.zeros(logits.shape, F32)
    member = jnp.zeros(logits.shape, F32)
    sels = []
    m0 = None
    denom = None
    for kk in range(TOP_K):
        m = logits.max(axis=-1, keepdims=True)
        sel = jnp.min(jnp.where(logits == m, lane, LANES), axis=-1, keepdims=True)
        sels.append(sel)
        if kk == 0:
            m0 = m
        e = jnp.exp(m - m0)
        denom = e if denom is None else denom + e
        idx_out = jnp.where(lane == kk, sel, idx_out)
        val_out = jnp.where(lane == kk, e, val_out)
        member = jnp.where(lane == sel, 1.0, member)
        logits = jnp.where(lane == sel, -jnp.inf, logits)
    idx_ref[...] = idx_out
    gate_ref[...] = val_out / denom

    ti = lax.broadcasted_iota(jnp.int32, (ROUTER_TM, ROUTER_TM), 0)
    tj = lax.broadcasted_iota(jnp.int32, (ROUTER_TM, ROUTER_TM), 1)
    earlier = jnp.where(tj < ti, 1.0, 0.0).astype(BF16)
    before = _dot(earlier, member.astype(BF16)) + run_scr[...]
    rank_out = jnp.zeros(logits.shape, F32)
    for kk in range(TOP_K):
        r = jnp.sum(jnp.where(lane == sels[kk], before, 0.0), axis=-1, keepdims=True)
        rank_out = jnp.where(lane == kk, r, rank_out)
    rank_ref[...] = rank_out.astype(jnp.int32)
    run_scr[...] = run_scr[...] + jnp.sum(member, axis=0, keepdims=True)
    cnt_ref[...] = run_scr[...]


def _router(h2, w_pad, b_pad):
    tm = ROUTER_TM
    tok_spec = pl.BlockSpec((tm, LANES), lambda i: (i, 0))
    return pl.pallas_call(
        _router_kernel,
        grid=(N_TOK // tm,),
        in_specs=[pl.BlockSpec((tm, D_MODEL), lambda i: (i, 0)),
                  pl.BlockSpec((D_MODEL, LANES), lambda i: (0, 0)),
                  pl.BlockSpec((1, LANES), lambda i: (0, 0))],
        out_specs=[tok_spec, tok_spec, tok_spec, pl.BlockSpec((1, LANES), lambda i: (0, 0))],
        out_shape=[jax.ShapeDtypeStruct((N_TOK, LANES), jnp.int32), jax.ShapeDtypeStruct((N_TOK, LANES), F32),
                   jax.ShapeDtypeStruct((N_TOK, LANES), jnp.int32), jax.ShapeDtypeStruct((1, LANES), F32)],
        scratch_shapes=[pltpu.VMEM((1, LANES), F32)],
        compiler_params=_cparams(("arbitrary",), 32),
        name="router")(h2, w_pad, b_pad)


def _tile_idx(i, nu_ref):
    return jnp.minimum(i, nu_ref[0] - 1)


def _run_start(te_ref, i):
    prev = te_ref[jnp.maximum(i - 1, 0)]
    return (i == 0) | (te_ref[i] != prev)


def _stream_weights(te_ref, nx_ref, copy_for, n_copies, on_ready):
    j = pl.program_id(0)
    i = pl.program_id(1)
    e = te_ref[i]

    @pl.when((i == 0) & (j == 0))
    def _():
        for c in range(n_copies):
            copy_for(e, j, c).start()

    for c in range(n_copies):
        copy_for(e, j, c).wait()
    on_ready()
    ne = nx_ref[e]
    jn = j + (ne <= e).astype(jnp.int32)

    @pl.when(jn < pl.num_programs(0))
    def _():
        for c in range(n_copies):
            copy_for(ne, jn, c).start()


def _moe_gu_kernel(te_ref, nu_ref, nx_ref, x_ref, w_hbm, bg_ref, bu_ref, h_ref, stage, wg_s, wu_s, sems, *, layer):
    i = pl.program_id(1)

    def copy_for(e, jj, which):
        col = pl.multiple_of(which * EXPERT_FF + jj * MOE_TF, MOE_TF)
        return pltpu.make_async_copy(w_hbm.at[layer, e, :, pl.ds(col, MOE_TF)], stage.at[which], sems.at[which])

    def cast():
        wg_s[...] = stage[0].astype(BF16)
        wu_s[...] = stage[1].astype(BF16)

    @pl.when(i < nu_ref[0])
    def _():
        @pl.when(_run_start(te_ref, i))
        def _():
            _stream_weights(te_ref, nx_ref, copy_for, 2, cast)

        x = x_ref[...]
        gate = _dot(x, wg_s[...]) + bg_ref[...]
        up = _dot(x, wu_s[...]) + bu_ref[...]
        gate = jnp.minimum(gate, SWIGLU_LIMIT)
        up = jnp.clip(up, -SWIGLU_LIMIT, SWIGLU_LIMIT)
        glu = gate * _sigmoid(SWIGLU_ALPHA * gate)
        h_ref[...] = ((up + 1.0) * glu).astype(BF16)

    @pl.when(i >= nu_ref[0])
    def _():
        h_ref[...] = jnp.zeros(h_ref.shape, BF16)


def _moe_gu(xs, w_gu, b_gu4, layer, tile_expert, n_used, next_expert):
    nj = EXPERT_FF // MOE_TF
    grid_spec = pltpu.PrefetchScalarGridSpec(
        num_scalar_prefetch=3,
        grid=(nj, MOE_TILES),
        in_specs=[
            pl.BlockSpec((MOE_TM, D_MODEL), lambda j, i, te, nu, nx: (_tile_idx(i, nu), 0)),
            pl.BlockSpec(memory_space=pl.ANY),
            pl.BlockSpec((None, None, 1, MOE_TF), lambda j, i, te, nu, nx: (layer, te[_tile_idx(i, nu)], 0, j)),
            pl.BlockSpec((None, None, 1, MOE_TF), lambda j, i, te, nu, nx: (layer, te[_tile_idx(i, nu)], 0, nj + j)),
        ],
        out_specs=pl.BlockSpec((MOE_TM, MOE_TF), lambda j, i, te, nu, nx: (i, j)),
        scratch_shapes=[pltpu.VMEM((2, D_MODEL, MOE_TF), F32), pltpu.VMEM((D_MODEL, MOE_TF), BF16),
                        pltpu.VMEM((D_MODEL, MOE_TF), BF16), pltpu.SemaphoreType.DMA((2,))])
    return pl.pallas_call(
        functools.partial(_moe_gu_kernel, layer=layer), grid_spec=grid_spec,
        out_shape=jax.ShapeDtypeStruct((MOE_SLOTS, EXPERT_FF), BF16),
        compiler_params=_cparams(("arbitrary", "arbitrary"), 44),
        name="moe_gate_up")(tile_expert, n_used, next_expert, xs, w_gu, b_gu4, b_gu4)


def _moe_down_kernel(te_ref, nu_ref, nx_ref, h_ref, w_hbm, b_ref, y_ref, stage, w_s, sems, *, layer):
    i = pl.program_id(1)

    def copy_for(e, jj, which):
        col = pl.multiple_of(jj * MOE_TN, MOE_TN)
        return pltpu.make_async_copy(w_hbm.at[layer, e, :, pl.ds(col, MOE_TN)], stage, sems.at[0])

    def cast():
        w_s[...] = stage[...].astype(BF16)

    @pl.when(i < nu_ref[0])
    def _():
        @pl.when(_run_start(te_ref, i))
        def _():
            _stream_weights(te_ref, nx_ref, copy_for, 1, cast)

        y = _dot(h_ref[...], w_s[...]) + b_ref[...]
        for c in range(Y_TILES):
            y_ref[pl.ds(c, MOE_TM, stride=Y_TILES), :] = y[:, c * LANES:(c + 1) * LANES]

    @pl.when(i >= nu_ref[0])
    def _():
        y_ref[...] = jnp.zeros(y_ref.shape, F32)


def _moe_down(hs, w_down, b_down4, layer, tile_expert, n_used, next_expert):
    grid_spec = pltpu.PrefetchScalarGridSpec(
        num_scalar_prefetch=3,
        grid=(Y_PARTS, MOE_TILES),
        in_specs=[
            pl.BlockSpec((MOE_TM, EXPERT_FF), lambda j, i, te, nu, nx: (_tile_idx(i, nu), 0)),
            pl.BlockSpec(memory_space=pl.ANY),
            pl.BlockSpec((None, None, 1, MOE_TN), lambda j, i, te, nu, nx: (layer, te[_tile_idx(i, nu)], 0, j)),
        ],
        out_specs=pl.BlockSpec((None, MOE_TM * Y_TILES, LANES), lambda j, i, te, nu, nx: (j, i, 0)),
        scratch_shapes=[pltpu.VMEM((EXPERT_FF, MOE_TN), F32), pltpu.VMEM((EXPERT_FF, MOE_TN), BF16),
                        pltpu.SemaphoreType.DMA((1,))])
    ys = pl.pallas_call(
        functools.partial(_moe_down_kernel, layer=layer), grid_spec=grid_spec,
        out_shape=jax.ShapeDtypeStruct((Y_PARTS, MOE_SLOTS * Y_TILES, LANES), F32),
        compiler_params=_cparams(("arbitrary", "arbitrary"), 44),
        name="moe_down")(tile_expert, n_used, next_expert, hs, w_down, b_down4)
    return ys.reshape(Y_PARTS, MOE_SLOTS, Y_TILES, LANES)


def _moe_plan(top_idx, rank, counts):
    experts = jnp.arange(N_EXPERTS, dtype=jnp.int32)
    padded = (counts + MOE_TM - 1) // MOE_TM * MOE_TM
    pad_end = jnp.cumsum(padded)
    pad_start = pad_end - padded
    onehot = top_idx[:, :, None] == experts[None, None, :]
    dest = jnp.sum(jnp.where(onehot, pad_start[None, None, :], 0), axis=-1).astype(jnp.int32) + rank
    tok = jnp.repeat(jnp.arange(N_TOK, dtype=jnp.int32), TOP_K)
    slot_tok = jnp.zeros((MOE_SLOTS,), jnp.int32).at[dest.reshape(-1)].set(tok, unique_indices=True)
    n_used = (pad_end[-1] // MOE_TM).astype(jnp.int32).reshape(1)
    tile_start = jnp.arange(MOE_TILES, dtype=jnp.int32) * MOE_TM
    tile_expert = jnp.minimum(jnp.sum(pad_end[None, :] <= tile_start[:, None], axis=1), N_EXPERTS - 1)
    has = counts > 0
    later = has[None, :] & (experts[None, :] > experts[:, None])
    nxt_later = jnp.min(jnp.where(later, experts[None, :], N_EXPERTS), axis=1)
    first = jnp.min(jnp.where(has, experts, N_EXPERTS))
    next_expert = jnp.where(nxt_later < N_EXPERTS, nxt_later, first).astype(jnp.int32)
    return slot_tok, dest, tile_expert.astype(jnp.int32), n_used, next_expert


def _moe(h2, h2_rows, lw, weights, layer):
    idx_pad, gate_pad, rank_pad, cnt = _router(h2, lw['router_w'], lw['router_b'])
    top_idx = idx_pad[:, :TOP_K]
    rank = rank_pad[:, :TOP_K]
    counts = cnt[0, :N_EXPERTS].astype(jnp.int32)
    slot_tok, dest, tile_expert, n_used, next_expert = _moe_plan(top_idx, rank, counts)
    xs = _gather_x(h2_rows, slot_tok)
    hs = _moe_gu(xs, weights['w_gu'], weights['b_gu'], layer, tile_expert, n_used, next_expert)
    ys = _moe_down(hs, weights['w_down'], weights['b_down'], layer, tile_expert, n_used, next_expert)
    return ys, dest, gate_pad


def _layer(x, cond16, lw, weights, layer, caches, tabs):
    state_gla, state_ret, cache_gqa_k, cache_gqa_v, cache_nat_k, cache_nat_v = caches
    mod = _adaln_mod(cond16, weights['mod_w'], weights['mod_b'], layer)
    mod3 = mod.reshape(16 * 6, 1, D_MODEL)

    h = _ln_mod(x, mod3)
    proj = _in_proj(h, lw['w_in'], tabs['zero_bias'])

    ya_c, st_a = _gla(proj, lw['gla_up'], lw['gla_ub'], lw['gla_norm_g'], latent=False)
    ya_l = _gla(proj, lw['gla_up'], lw['gla_ub'], lw['gla_norm_g'], latent=True, state=state_gla, layer=layer)[0]
    yb_c, st_b = _ret(proj, tabs['ret_lg'], lw['ret_gn_g'], lw['ret_gn_b'], latent=False)
    yb_l = _ret(proj, tabs['ret_lg'], lw['ret_gn_g'], lw['ret_gn_b'], latent=True,
                cos=tabs['cos64'], sin=tabs['sin64'], state=state_ret, layer=layer)[0]
    yc_c, kc = _ctx_attn(proj, COL_C_Q, COL_C_K, COL_C_V, GQA_KV_HEADS, GQA_HEADS // GQA_KV_HEADS,
                         lw['gqa_qn_g'], lw['gqa_kn_g'], norm=True, name="gqa_ctx")
    yc_l = _gqa_lat(proj, cache_gqa_k, cache_gqa_v, layer, lw['gqa_qn_g'], lw['gqa_kn_g'],
                    tabs['cos128'], tabs['sin128'])
    yd_c = _ctx_attn(proj, COL_D_Q, COL_D_K, COL_D_V, NAT_HEADS, 1, lw['gqa_qn_g'], lw['gqa_kn_g'],
                     norm=False, name="nat_ctx")[0]
    yd_l = _nat_lat(proj, cache_nat_k, cache_nat_v, layer, lw['nat_bias'])

    ys = [jnp.concatenate([c_, l_], axis=0) for c_, l_ in ((ya_c, ya_l), (yb_c, yb_l), (yc_c, yc_l), (yd_c, yd_l))]
    mix = _out_proj(ys, weights['w_o'], layer)
    x1, h2, h2_rows = _res_ln1(x, mix, mod3, lw['ln1_g'], lw['ln1_b'])
    y_rows, dest, gate_pad = _moe(h2, h2_rows, lw, weights, layer)
    x2 = _res_ln2(x1, y_rows, dest, gate_pad, mod3, lw['ln2_g'], lw['ln2_b'])

    vc = proj[:N_PROMPT, COL_C_V:COL_C_V + GQA_KV_HEADS * GQA_DIM]
    kd = proj[:N_PROMPT, COL_D_K:COL_D_K + NAT_HEADS * NAT_DIM]
    vd = proj[:N_PROMPT, COL_D_V:COL_D_V + NAT_HEADS * NAT_DIM]
    new = (st_a, st_b,
           kc.reshape(BATCH, SEQ, GQA_KV_HEADS, GQA_DIM), vc.reshape(BATCH, SEQ, GQA_KV_HEADS, GQA_DIM),
           kd.reshape(BATCH, SEQ, NAT_HEADS, NAT_DIM), vd.reshape(BATCH, SEQ, NAT_HEADS, NAT_DIM))
    return x2, new


def _prep_layer_weights(l, w_in, gla_lr_up, gla_lr_b, gla_norm_g, ret_gn_g, ret_gn_b, gqa_qn_g, gqa_kn_g, nat_rpb,
                        ln1_g, ln1_b, router_w, router_b, ln2_g, ln2_b):
    wi = w_in[l]
    w_in_p = jnp.concatenate([wi[:, :ALR_SRC], wi[:, ALR_SRC + ALR_W:], wi[:, ALR_SRC:ALR_SRC + ALR_W],
                              jnp.zeros((D_MODEL, PROJ_PAD - wi.shape[1]), wi.dtype)], axis=1).astype(BF16)
    up = jnp.zeros((2, LANES, GLA_HEADS * GLA_DK), F32)
    for d in range(2):
        up = up.at[d, d * GLA_RANK:(d + 1) * GLA_RANK, :].set(gla_lr_up[l, d])
    rw = jnp.zeros((D_MODEL, LANES), F32).at[:, :N_EXPERTS].set(router_w[l])
    rb = jnp.full((1, LANES), NEG_BIG, F32).at[0, :N_EXPERTS].set(router_b[l])
    return {
        'w_in': w_in_p,
        'gla_up': up, 'gla_ub': gla_lr_b[l].reshape(2, 1, -1), 'gla_norm_g': gla_norm_g[l].reshape(1, -1),
        'ret_gn_g': ret_gn_g[l].reshape(1, -1), 'ret_gn_b': ret_gn_b[l].reshape(1, -1),
        'gqa_qn_g': gqa_qn_g[l].reshape(1, -1), 'gqa_kn_g': gqa_kn_g[l].reshape(1, -1),
        'nat_bias': _nat_bias_table(nat_rpb[l]),
        'ln1_g': ln1_g[l].reshape(1, -1), 'ln1_b': ln1_b[l].reshape(1, -1),
        'router_w': rw, 'router_b': rb,
        'ln2_g': ln2_g[l].reshape(1, -1), 'ln2_b': ln2_b[l].reshape(1, -1),
    }


def kernel(x_prompt, x_sample, state_gla, state_ret, cache_gqa_k, cache_gqa_v, cache_nat_k, cache_nat_v, c, c_ctx, mod_w, mod_b, w_in, gla_lr_up, gla_lr_b, gla_norm_g, ret_gn_g, ret_gn_b, gqa_qn_g, gqa_kn_g, nat_rpb, w_o, ln1_g, ln1_b, router_w, router_b, w_gu, b_gu, w_down, b_down, ln2_g, ln2_b):
    x = jnp.concatenate([x_prompt.reshape(N_PROMPT, D_MODEL), x_sample.reshape(N_SAMPLE, D_MODEL)], axis=0)
    cond16 = jnp.zeros((16, D_MODEL), F32).at[0].set(c_ctx).at[1:1 + DEC_BATCH].set(c)
    cos128, sin128 = _rope_tables(DEC_SEQ, GQA_DIM // 4)
    cos64, sin64 = _rope_tables(DEC_SEQ, RET_DK // 4)
    tabs = {'cos128': cos128, 'sin128': sin128, 'cos64': cos64, 'sin64': sin64,
            'ret_lg': jnp.asarray(_ret_log_decays(), F32), 'zero_bias': jnp.zeros((1, PROJ_PAD), F32)}
    weights = {'mod_w': mod_w, 'mod_b': mod_b.reshape(DEPTH, 1, -1), 'w_o': w_o,
               'w_gu': w_gu, 'b_gu': b_gu.reshape(DEPTH, N_EXPERTS, 1, -1),
               'w_down': w_down, 'b_down': b_down.reshape(DEPTH, N_EXPERTS, 1, -1)}
    caches = (state_gla.reshape(DEC_BATCH, DEPTH, 2, GLA_HEADS // 2, 2 * GLA_DK, GLA_DV),
              state_ret.reshape(DEC_BATCH, DEPTH, 2, RET_HEADS // 2, 2 * RET_DK, RET_DV),
              cache_gqa_k.reshape(DEC_BATCH, DEPTH, PAST_LEN, GQA_KV_HEADS * GQA_DIM),
              cache_gqa_v.reshape(DEC_BATCH, DEPTH, PAST_LEN, GQA_KV_HEADS * GQA_DIM),
              cache_nat_k.reshape(DEC_BATCH, DEPTH, PAST_LEN, NAT_HEADS * NAT_DIM),
              cache_nat_v.reshape(DEC_BATCH, DEPTH, PAST_LEN, NAT_HEADS * NAT_DIM))
    news = []
    for l in range(DEPTH):
        lw = _prep_layer_weights(l, w_in, gla_lr_up, gla_lr_b, gla_norm_g, ret_gn_g, ret_gn_b, gqa_qn_g, gqa_kn_g,
                                 nat_rpb, ln1_g, ln1_b, router_w, router_b, ln2_g, ln2_b)
        x, new = _layer(x, cond16, lw, weights, l, caches, tabs)
        news.append(new)
    y_p = x[:N_PROMPT].reshape(BATCH, SEQ, D_MODEL)
    y_s = x[N_PROMPT:].reshape(DEC_BATCH, DEC_SEQ, D_MODEL)
    stacked = tuple(jnp.stack([news[l][k] for l in range(DEPTH)], axis=1) for k in range(6))
    return (y_p, y_s) + stacked
```

```python
import functools

import numpy as np
import jax
import jax.numpy as jnp
from jax import lax
from jax.experimental import pallas as pl
from jax.experimental.pallas import tpu as pltpu

F32 = jnp.float32
BF16 = jnp.bfloat16
U32 = jnp.uint32
I32 = jnp.int32

D_MODEL = 4096
BATCH = 16
SEQ = 256
DEPTH = 2
DEC_BATCH = 8
DEC_SEQ = 1024
PAST_LEN = 512
GRID_W = 64
GLA_HEADS = 8
GLA_DK = 64
GLA_DV = 128
GLA_RANK = 16
GLA_NORMALIZER = 16.0
RET_HEADS = 8
RET_DK = 64
RET_DV = 128
GQA_HEADS = 8
GQA_KV_HEADS = 2
GQA_DIM = 128
NAT_HEADS = 8
NAT_DIM = 128
NAT_ROWS = 8
NAT_COLS = 16
CHUNK = 64
ROPE_THETA = 10000.0
LN_EPS = 1e-6
N_EXPERTS = 32
TOP_K = 4
EXPERT_FF = 2048
SWIGLU_LIMIT = 7.0
SWIGLU_ALPHA = 1.702
DEEPNORM_ALPHA = (2.0 * DEPTH) ** 0.25

N_PROMPT = BATCH * SEQ
N_SAMPLE = DEC_BATCH * DEC_SEQ
N_TOK = N_PROMPT + N_SAMPLE

LANES = 128
SUBLANES = 8
ROW_TILE = 256
NEG_BIG = -1e30

COL_A_Q, COL_A_K, COL_A_V, COL_A_G = 0, 512, 1024, 2048
COL_B_Q, COL_B_K, COL_B_V, COL_B_G = 3072, 3584, 4096, 5120
COL_C_Q, COL_C_K, COL_C_V = 6144, 7168, 7424
COL_D_Q, COL_D_K, COL_D_V = 7680, 8704, 9728
COL_A_LR = 10752
PROJ_PAD = 10880
ALR_SRC = 3072
ALR_W = 2 * GLA_RANK

MOE_TM = 256
NK = N_TOK * TOP_K
MOE_TILES = NK // MOE_TM + N_EXPERTS
MOE_SLOTS = MOE_TILES * MOE_TM
MOE_TF = 512
MOE_TN = 2048
ROUTER_TM = 512

HALF = D_MODEL // 2
X_TILES = HALF // LANES
Y_PARTS = D_MODEL // MOE_TN
Y_TILES = MOE_TN // LANES
X_PITCH = X_TILES + SUBLANES
Y_PITCH = Y_TILES + SUBLANES


def _cparams(sem, vmem_mb):
    return pltpu.CompilerParams(dimension_semantics=sem, vmem_limit_bytes=int(vmem_mb * 1024 * 1024))


def _dot(a, b):
    return jnp.dot(a, b, preferred_element_type=F32)


def _dot_nt(a, b):
    return lax.dot_general(a, b, (((1,), (1,)), ((), ())), preferred_element_type=F32)


def _dot_tn(a, b):
    return lax.dot_general(a, b, (((0,), (0,)), ((), ())), preferred_element_type=F32)


def _sigmoid(x):
    return 1.0 / (1.0 + jnp.exp(-x))


def _silu(x):
    return x * _sigmoid(x)


def _ln_rows(x):
    mu = jnp.mean(x, axis=-1, keepdims=True)
    xc = x - mu
    var = jnp.mean(xc * xc, axis=-1, keepdims=True)
    return xc * lax.rsqrt(var + LN_EPS)


def _rms_rows(x):
    return x * lax.rsqrt(jnp.mean(x * x, axis=-1, keepdims=True) + LN_EPS)


def _pack_words(a, b):
    ab = pltpu.bitcast(a.astype(BF16).astype(F32), U32)
    bb = pltpu.bitcast(b.astype(BF16).astype(F32), U32)
    return (ab >> 16) | (bb & jnp.uint32(0xFFFF0000))


def _unpack_words(w):
    lo = pltpu.bitcast(w << 16, F32)
    hi = pltpu.bitcast(w & jnp.uint32(0xFFFF0000), F32)
    return lo, hi


def _mod_spec(which, tile):
    n_ctx_tiles = N_PROMPT // tile
    per_b = DEC_SEQ // tile

    def imap(i, *_):
        row = jnp.where(i < n_ctx_tiles, 0, 1 + (i - n_ctx_tiles) // per_b)
        return (row * 6 + which, 0, 0)
    return pl.BlockSpec((1, 1, D_MODEL), imap)


def _mm_kernel(a_ref, w_ref, b_ref, o_ref, *, silu_a):
    a = a_ref[...]
    if silu_a:
        a = _silu(a.astype(F32))
    acc = _dot(a.astype(BF16), w_ref[...].astype(BF16))
    o_ref[...] = acc + b_ref[...]


def _adaln_mod(cond16, mod_w, mod_b3, layer):
    k, n = mod_w.shape[1], mod_w.shape[2]
    tn = 512
    return pl.pallas_call(
        functools.partial(_mm_kernel, silu_a=True),
        grid=(1, n // tn),
        in_specs=[pl.BlockSpec((16, k), lambda i, j: (0, 0)),
                  pl.BlockSpec((None, k, tn), lambda i, j: (layer, 0, j)),
                  pl.BlockSpec((None, 1, tn), lambda i, j: (layer, 0, j))],
        out_specs=pl.BlockSpec((16, tn), lambda i, j: (0, j)),
        out_shape=jax.ShapeDtypeStruct((16, n), F32),
        compiler_params=_cparams(("arbitrary", "arbitrary"), 40),
        name="adaln_mod")(cond16, mod_w, mod_b3)


def _in_proj(h, w_in_p, zero_bias):
    tm, tn = 1024, 640
    k = h.shape[1]
    return pl.pallas_call(
        functools.partial(_mm_kernel, silu_a=False),
        grid=(N_TOK // tm, PROJ_PAD // tn),
        in_specs=[pl.BlockSpec((tm, k), lambda i, j: (i, 0)),
                  pl.BlockSpec((k, tn), lambda i, j: (0, j)),
                  pl.BlockSpec((1, tn), lambda i, j: (0, j))],
        out_specs=pl.BlockSpec((tm, tn), lambda i, j: (i, j)),
        out_shape=jax.ShapeDtypeStruct((N_TOK, PROJ_PAD), F32),
        compiler_params=_cparams(("arbitrary", "arbitrary"), 48),
        name="in_proj")(h, w_in_p, zero_bias)


def _mm4_kernel(*refs, n_ctx_tiles):
    ctx, lat, ws, o_ref = refs[0:4], refs[4:8], refs[8:12], refs[12]

    def project(parts):
        acc = _dot(parts[0][...], ws[0][...].astype(BF16))
        for a, w in zip(parts[1:], ws[1:]):
            acc += _dot(a[...], w[...].astype(BF16))
        o_ref[...] = acc

    is_ctx = pl.program_id(0) < n_ctx_tiles
    pl.when(is_ctx)(lambda: project(ctx))
    pl.when(jnp.logical_not(is_ctx))(lambda: project(lat))


def _out_proj(ys_ctx, ys_lat, w_o, layer, *, tm=1024, tn=256):
    kp = ys_ctx[0].shape[1]
    n_ctx_tiles = N_PROMPT // tm
    ctx_specs = [pl.BlockSpec((tm, kp), lambda i, j: (jnp.minimum(i, n_ctx_tiles - 1), 0)) for _ in range(4)]
    lat_specs = [pl.BlockSpec((tm, kp), lambda i, j: (jnp.maximum(i - n_ctx_tiles, 0), 0)) for _ in range(4)]
    w_specs = [pl.BlockSpec((None, kp, tn), functools.partial(lambda i, j, p: (layer, p, j), p=p)) for p in range(4)]
    return pl.pallas_call(
        functools.partial(_mm4_kernel, n_ctx_tiles=n_ctx_tiles),
        grid=(N_TOK // tm, D_MODEL // tn),
        in_specs=ctx_specs + lat_specs + w_specs,
        out_specs=pl.BlockSpec((tm, tn), lambda i, j: (i, j)),
        out_shape=jax.ShapeDtypeStruct((N_TOK, D_MODEL), F32),
        compiler_params=_cparams(("arbitrary", "arbitrary"), 48),
        name="out_proj")(*ys_ctx, *ys_lat, w_o, w_o, w_o, w_o)


def _ln_mod_kernel(x_ref, sh_ref, sc_ref, h_ref):
    h = _ln_rows(x_ref[...]) * (1.0 + sc_ref[0]) + sh_ref[0]
    h_ref[...] = h.astype(BF16)


def _ln_mod(x, mod3):
    return pl.pallas_call(
        _ln_mod_kernel,
        grid=(N_TOK // ROW_TILE,),
        in_specs=[pl.BlockSpec((ROW_TILE, D_MODEL), lambda i: (i, 0)), _mod_spec(0, ROW_TILE),
                  _mod_spec(1, ROW_TILE)],
        out_specs=pl.BlockSpec((ROW_TILE, D_MODEL), lambda i: (i, 0)),
        out_shape=jax.ShapeDtypeStruct((N_TOK, D_MODEL), BF16),
        compiler_params=_cparams(("arbitrary",), 32),
        name="ln_mod")(x, mod3, mod3)


def _res_ln1_kernel(x_ref, mix_ref, gate_ref, g_ref, b_ref, sh_ref, sc_ref, x1_ref, h2_ref, rows_ref):
    y = DEEPNORM_ALPHA * x_ref[...] + gate_ref[0] * mix_ref[...]
    x1 = _ln_rows(y) * g_ref[...] + b_ref[...]
    x1_ref[...] = x1
    h2 = _ln_rows(x1) * (1.0 + sc_ref[0]) + sh_ref[0]
    h2_ref[...] = h2.astype(BF16)
    for c in range(X_TILES):
        lo = h2[:, c * LANES:(c + 1) * LANES]
        hi = h2[:, HALF + c * LANES:HALF + (c + 1) * LANES]
        rows_ref[pl.ds(c, ROW_TILE, stride=X_TILES), :] = _pack_words(lo, hi)


def _res_ln1(x, mix, mod3, ln_g, ln_b):
    tile = ROW_TILE
    row_spec = pl.BlockSpec((tile, D_MODEL), lambda i: (i, 0))
    vec_spec = pl.BlockSpec((1, D_MODEL), lambda i: (0, 0))
    x1, h2, rows = pl.pallas_call(
        _res_ln1_kernel,
        grid=(N_TOK // tile,),
        in_specs=[row_spec, row_spec, _mod_spec(2, tile), vec_spec, vec_spec, _mod_spec(3, tile), _mod_spec(4, tile)],
        out_specs=[row_spec, row_spec, pl.BlockSpec((tile * X_TILES, LANES), lambda i: (i, 0))],
        out_shape=[jax.ShapeDtypeStruct((N_TOK, D_MODEL), F32), jax.ShapeDtypeStruct((N_TOK, D_MODEL), BF16),
                   jax.ShapeDtypeStruct((N_TOK * X_TILES, LANES), U32)],
        compiler_params=_cparams(("arbitrary",), 48),
        name="res_ln_mixer")(x, mix, mod3, ln_g, ln_b, mod3, mod3)
    return x1, h2, rows.reshape(1, N_TOK, X_TILES, LANES)


def _gather_step(idx_cur, idx_nxt, src_ref, buf, sems, n_rows, tiles, pitch):
    step = pl.program_id(0)
    slot = step % 2

    def issue(idx_ref, to_slot):
        def body(pair, carry):
            for queue in range(2):
                r = pair * 2 + queue
                dst = buf.at[to_slot, :, pl.ds(pl.multiple_of(r * pitch, SUBLANES), tiles), :]
                pltpu.make_async_copy(src_ref.at[:, idx_ref[0, 0, r]], dst, sems.at[to_slot]).start(priority=queue)
            return carry
        lax.fori_loop(0, n_rows // 2, body, 0, unroll=4)

    @pl.when(step == 0)
    def _():
        issue(idx_cur, slot)

    @pl.when(step + 1 < pl.num_programs(0))
    def _():
        issue(idx_nxt, 1 - slot)

    landed = buf.at[slot, :, pl.ds(0, n_rows * tiles), :]
    pltpu.make_async_copy(landed, landed, sems.at[slot]).wait()
    return slot


def _idx_specs(n_rows, n_steps):
    return [pl.BlockSpec((1, 1, n_rows), lambda i: (i, 0, 0), memory_space=pltpu.SMEM),
            pl.BlockSpec((1, 1, n_rows), lambda i: (jnp.minimum(i + 1, n_steps - 1), 0, 0),
                         memory_space=pltpu.SMEM)]


def _res_ln2_kernel(idx_cur, idx_nxt, x_ref, y_hbm, gates_ref, gate_ref, g_ref, b_ref, o_ref, buf, sems, mix_scr,
                    *, tile):
    slot = _gather_step(idx_cur, idx_nxt, y_hbm, buf, sems, TOP_K * tile, Y_TILES, Y_PITCH)
    gates = gates_ref[...]
    for part in range(Y_PARTS):
        for c in range(Y_TILES):
            acc = None
            for kk in range(TOP_K):
                slab = buf[slot, part, pl.ds(kk * tile * Y_PITCH + c, tile, stride=Y_PITCH), :]
                term = gates[:, kk:kk + 1] * slab
                acc = term if acc is None else acc + term
            col = part * MOE_TN + c * LANES
            mix_scr[:, col:col + LANES] = acc
    y = DEEPNORM_ALPHA * x_ref[...] + gate_ref[0] * mix_scr[...]
    o_ref[...] = _ln_rows(y) * g_ref[...] + b_ref[...]


def _res_ln2(x1, y_rows, dest, gates_pad, mod3, ln_g, ln_b):
    tile = 64
    n_tiles = N_TOK // tile
    idx = dest.reshape(n_tiles, tile, TOP_K).transpose(0, 2, 1).reshape(n_tiles, 1, TOP_K * tile)
    row_spec = pl.BlockSpec((tile, D_MODEL), lambda i: (i, 0))
    vec_spec = pl.BlockSpec((1, D_MODEL), lambda i: (0, 0))
    return pl.pallas_call(
        functools.partial(_res_ln2_kernel, tile=tile),
        grid=(n_tiles,),
        in_specs=_idx_specs(TOP_K * tile, n_tiles) + [row_spec, pl.BlockSpec(memory_space=pl.ANY),
                                                      pl.BlockSpec((tile, LANES), lambda i: (i, 0)),
                                                      _mod_spec(5, tile), vec_spec, vec_spec],
        out_specs=row_spec,
        out_shape=jax.ShapeDtypeStruct((N_TOK, D_MODEL), F32),
        scratch_shapes=[pltpu.VMEM((2, Y_PARTS, TOP_K * tile * Y_PITCH, LANES), F32), pltpu.SemaphoreType.DMA((2,)),
                        pltpu.VMEM((tile, D_MODEL), F32)],
        compiler_params=_cparams(("arbitrary",), 40),
        name="res_ln_moe")(idx, idx, x1, y_rows, gates_pad, mod3, ln_g, ln_b)


def _gather_x_kernel(idx_cur, idx_nxt, src_hbm, o_ref, buf, sems):
    slot = _gather_step(idx_cur, idx_nxt, src_hbm, buf, sems, MOE_TM, X_TILES, X_PITCH)
    for c in range(X_TILES):
        lo, hi = _unpack_words(buf[slot, 0, pl.ds(c, MOE_TM, stride=X_PITCH), :])
        o_ref[:, c * LANES:(c + 1) * LANES] = lo.astype(BF16)
        o_ref[:, HALF + c * LANES:HALF + (c + 1) * LANES] = hi.astype(BF16)


def _gather_x(h2_rows, slot_tok):
    idx = slot_tok.reshape(MOE_TILES, 1, MOE_TM)
    return pl.pallas_call(
        _gather_x_kernel,
        grid=(MOE_TILES,),
        in_specs=_idx_specs(MOE_TM, MOE_TILES) + [pl.BlockSpec(memory_space=pl.ANY)],
        out_specs=pl.BlockSpec((MOE_TM, D_MODEL), lambda i: (i, 0)),
        out_shape=jax.ShapeDtypeStruct((MOE_SLOTS, D_MODEL), BF16),
        scratch_shapes=[pltpu.VMEM((2, 1, MOE_TM * X_PITCH, LANES), U32), pltpu.SemaphoreType.DMA((2,))],
        compiler_params=_cparams(("arbitrary",), 32),
        name="moe_gather_x")(idx, idx, h2_rows)


def _rope(x, cos, sin_signed, quarter):
    lane = lax.broadcasted_iota(jnp.int32, x.shape, 1)
    up = pltpu.roll(x, LANES - quarter, axis=1)
    dn = pltpu.roll(x, quarter, axis=1)
    partner = jnp.where((lane % (2 * quarter)) < quarter, up, dn)
    return x * cos + partner * sin_signed


def _rope_tables(t_len, quarter):
    lane = np.arange(LANES)
    within = lane % (4 * quarter)
    is_col = within >= 2 * quarter
    j = within % quarter
    second = (within % (2 * quarter)) >= quarter
    freqs = ROPE_THETA ** (-(j.astype(np.float32)) / quarter)
    t = np.arange(t_len)
    pos = np.where(is_col[None, :], (t % GRID_W)[:, None], (t // GRID_W)[:, None]).astype(np.float32)
    ang = pos * freqs[None, :].astype(np.float32)
    cos = np.cos(ang).astype(np.float32)
    sin = np.sin(ang).astype(np.float32) * np.where(second, 1.0, -1.0)[None, :].astype(np.float32)
    return jnp.asarray(cos), jnp.asarray(sin)


def _attend(q, keys, vals, scale, biases=None):
    ss = []
    for idx, k in enumerate(keys):
        s = _dot_nt(q, k) * scale
        if biases is not None and biases[idx] is not None:
            s = s + biases[idx]
        ss.append(s)
    m = ss[0].max(axis=-1, keepdims=True)
    for s in ss[1:]:
        m = jnp.maximum(m, s.max(axis=-1, keepdims=True))
    l = None
    o = None
    for s, v in zip(ss, vals):
        p = jnp.exp(s - m)
        pl_sum = p.sum(axis=-1, keepdims=True)
        po = _dot(p.astype(BF16), v)
        l = pl_sum if l is None else l + pl_sum
        o = po if o is None else o + po
    return o / l


def _ctx_attn_kernel(q_ref, k_ref, v_ref, gq_ref, gk_ref, o_ref, *rest, groups, norm):
    k = k_ref[...]
    if norm:
        k = _rms_rows(k) * gk_ref[...]
        rest[0][...] = k
    kb = k.astype(BF16)
    vb = v_ref[...].astype(BF16)
    scale = GQA_DIM ** -0.5
    for g in range(groups):
        q = q_ref[:, g * LANES:(g + 1) * LANES]
        if norm:
            q = _rms_rows(q) * gq_ref[...]
        o_ref[:, g * LANES:(g + 1) * LANES] = _attend(q.astype(BF16), [kb], [vb], scale).astype(BF16)


def _ctx_attn(proj, col_q, col_k, col_v, n_kv, groups, gq, gk, *, norm, name):
    qw = groups * LANES
    in_specs = [pl.BlockSpec((SEQ, qw), lambda b, h: (b, col_q // qw + h)),
                pl.BlockSpec((SEQ, LANES), lambda b, h: (b, col_k // LANES + h)),
                pl.BlockSpec((SEQ, LANES), lambda b, h: (b, col_v // LANES + h)),
                pl.BlockSpec((1, LANES), lambda b, h: (0, 0)),
                pl.BlockSpec((1, LANES), lambda b, h: (0, 0))]
    out_specs = [pl.BlockSpec((SEQ, qw), lambda b, h: (b, h))]
    out_shape = [jax.ShapeDtypeStruct((N_PROMPT, n_kv * qw), BF16)]
    if norm:
        out_specs.append(pl.BlockSpec((SEQ, LANES), lambda b, h: (b, h)))
        out_shape.append(jax.ShapeDtypeStruct((N_PROMPT, n_kv * LANES), F32))
    return pl.pallas_call(
        functools.partial(_ctx_attn_kernel, groups=groups, norm=norm),
        grid=(BATCH, n_kv),
        in_specs=in_specs, out_specs=out_specs, out_shape=out_shape,
        compiler_params=_cparams(("arbitrary", "arbitrary"), 32),
        name=name)(proj, proj, proj, gq, gk)


def _gqa_lat_kernel(q_ref, k_ref, v_ref, ck_ref, cv_ref, gq_ref, gk_ref, cosq_ref, sinq_ref, cosk_ref, sink_ref,
                    o_ref, k_scr, v_scr):
    qb = pl.program_id(2)

    @pl.when(qb == 0)
    def _():
        k = _rms_rows(k_ref[...]) * gk_ref[...]
        k_scr[...] = _rope(k, cosk_ref[...], sink_ref[...], GQA_DIM // 4).astype(BF16)
        v_scr[...] = v_ref[...].astype(BF16)

    ck = ck_ref[...].astype(BF16)
    cv = cv_ref[...].astype(BF16)
    scale = GQA_DIM ** -0.5
    for g in range(GQA_HEADS // GQA_KV_HEADS):
        q = _rms_rows(q_ref[:, g * LANES:(g + 1) * LANES]) * gq_ref[...]
        q = _rope(q, cosq_ref[...], sinq_ref[...], GQA_DIM // 4).astype(BF16)
        o = _attend(q, [k_scr[...], ck], [v_scr[...], cv], scale)
        o_ref[:, g * LANES:(g + 1) * LANES] = o.astype(BF16)


def _gqa_lat(proj, cache_k, cache_v, layer, gq, gk, cos, sin):
    groups = GQA_HEADS // GQA_KV_HEADS
    qw = groups * LANES
    tq = ROW_TILE
    nqb = DEC_SEQ // tq
    row0 = N_PROMPT // tq
    in_specs = [
        pl.BlockSpec((tq, qw), lambda b, h, i: (row0 + b * nqb + i, COL_C_Q // qw + h)),
        pl.BlockSpec((DEC_SEQ, LANES), lambda b, h, i: (N_PROMPT // DEC_SEQ + b, COL_C_K // LANES + h)),
        pl.BlockSpec((DEC_SEQ, LANES), lambda b, h, i: (N_PROMPT // DEC_SEQ + b, COL_C_V // LANES + h)),
        pl.BlockSpec((None, None, PAST_LEN, LANES), lambda b, h, i: (b, layer, 0, h)),
        pl.BlockSpec((None, None, PAST_LEN, LANES), lambda b, h, i: (b, layer, 0, h)),
        pl.BlockSpec((1, LANES), lambda b, h, i: (0, 0)),
        pl.BlockSpec((1, LANES), lambda b, h, i: (0, 0)),
        pl.BlockSpec((tq, LANES), lambda b, h, i: (i, 0)),
        pl.BlockSpec((tq, LANES), lambda b, h, i: (i, 0)),
        pl.BlockSpec((DEC_SEQ, LANES), lambda b, h, i: (0, 0)),
        pl.BlockSpec((DEC_SEQ, LANES), lambda b, h, i: (0, 0)),
    ]
    return pl.pallas_call(
        _gqa_lat_kernel,
        grid=(DEC_BATCH, GQA_KV_HEADS, nqb),
        in_specs=in_specs,
        out_specs=pl.BlockSpec((tq, qw), lambda b, h, i: (b * nqb + i, h)),
        out_shape=jax.ShapeDtypeStruct((N_SAMPLE, GQA_HEADS * GQA_DIM), BF16),
        scratch_shapes=[pltpu.VMEM((DEC_SEQ, LANES), BF16), pltpu.VMEM((DEC_SEQ, LANES), BF16)],
        compiler_params=_cparams(("arbitrary", "arbitrary", "arbitrary"), 40),
        name="gqa_latent")(proj, proj, proj, cache_k, cache_v, gq, gk, cos, sin, cos, sin)


NAT_GRID_ROWS = DEC_SEQ // GRID_W
NAT_KR = min(NAT_ROWS, NAT_GRID_ROWS)
NAT_BAND = NAT_KR * GRID_W
NAT_PATTERNS = NAT_KR


def _nat_lat_kernel(q_ref, k_ref, v_ref, ck_ref, cv_ref, bias_ref, o_ref, k_scr, v_scr, ck_scr, cv_scr):
    k_scr[...] = k_ref[...].astype(BF16)
    v_scr[...] = v_ref[...].astype(BF16)
    ck_scr[...] = ck_ref[...].astype(BF16)
    cv_scr[...] = cv_ref[...].astype(BF16)
    scale = NAT_DIM ** -0.5

    def body(r, carry):
        band = jnp.clip(r - NAT_KR // 2, 0, NAT_GRID_ROWS - NAT_KR)
        start = pl.multiple_of(band * GRID_W, GRID_W)
        rows = pl.ds(pl.multiple_of(r * GRID_W, GRID_W), GRID_W)
        q = q_ref[rows, :].astype(BF16)
        kb = k_scr[pl.ds(start, NAT_BAND), :]
        vb = v_scr[pl.ds(start, NAT_BAND), :]
        bias = bias_ref[r - band]
        o = _attend(q, [kb, ck_scr[...]], [vb, cv_scr[...]], scale, biases=[bias, None])
        o_ref[rows, :] = o.astype(BF16)
        return carry

    lax.fori_loop(0, NAT_GRID_ROWS, body, 0, unroll=4)


def _nat_lat(proj, cache_k, cache_v, layer, bias_tab):
    rb = N_PROMPT // DEC_SEQ
    in_specs = [
        pl.BlockSpec((DEC_SEQ, LANES), lambda b, h: (rb + b, COL_D_Q // LANES + h)),
        pl.BlockSpec((DEC_SEQ, LANES), lambda b, h: (rb + b, COL_D_K // LANES + h)),
        pl.BlockSpec((DEC_SEQ, LANES), lambda b, h: (rb + b, COL_D_V // LANES + h)),
        pl.BlockSpec((None, None, PAST_LEN, LANES), lambda b, h: (b, layer, 0, h)),
        pl.BlockSpec((None, None, PAST_LEN, LANES), lambda b, h: (b, layer, 0, h)),
        pl.BlockSpec((None, NAT_PATTERNS, GRID_W, NAT_BAND), lambda b, h: (h, 0, 0, 0)),
    ]
    return pl.pallas_call(
        _nat_lat_kernel,
        grid=(DEC_BATCH, NAT_HEADS),
        in_specs=in_specs,
        out_specs=pl.BlockSpec((DEC_SEQ, LANES), lambda b, h: (b, h)),
        out_shape=jax.ShapeDtypeStruct((N_SAMPLE, NAT_HEADS * NAT_DIM), BF16),
        scratch_shapes=[pltpu.VMEM((DEC_SEQ, LANES), BF16), pltpu.VMEM((DEC_SEQ, LANES), BF16),
                        pltpu.VMEM((PAST_LEN, LANES), BF16), pltpu.VMEM((PAST_LEN, LANES), BF16)],
        compiler_params=_cparams(("arbitrary", "arbitrary"), 32),
        name="nat_latent")(proj, proj, proj, cache_k, cache_v, bias_tab)


def _nat_bias_table(rpb):
    off = np.arange(NAT_PATTERNS)
    kr = np.arange(NAT_KR)
    rel_r = kr[None, :] - off[:, None] + NAT_ROWS - 1
    c = np.arange(GRID_W)
    rel_c = np.clip(c[None, :] - c[:, None], 1 - NAT_COLS, NAT_COLS - 1) + NAT_COLS - 1
    col_start = np.clip(c - NAT_COLS // 2, 0, GRID_W - NAT_COLS)
    col_ok = (c[None, :] >= col_start[:, None]) & (c[None, :] < col_start[:, None] + NAT_COLS)
    n_pk = NAT_PATTERNS * NAT_KR
    row_sel = np.zeros((n_pk, 2 * NAT_ROWS - 1), np.float32)
    row_sel[np.arange(n_pk), rel_r.reshape(-1)] = 1.0
    col_sel = np.zeros((2 * NAT_COLS - 1, GRID_W * GRID_W), np.float32)
    col_sel[rel_c.reshape(-1), np.arange(GRID_W * GRID_W)] = 1.0
    bias = jnp.einsum('ka,hab,bx->hkx', jnp.asarray(row_sel), rpb.astype(F32), jnp.asarray(col_sel),
                      precision=lax.Precision.HIGHEST)
    bias = bias.reshape(NAT_HEADS, NAT_PATTERNS, NAT_KR, GRID_W, GRID_W).transpose(0, 1, 3, 2, 4)
    bias = jnp.where(jnp.asarray(col_ok)[None, None, :, None, :], bias, NEG_BIG)
    return bias.reshape(NAT_HEADS, NAT_PATTERNS, GRID_W, NAT_BAND)


def _ret_log_decays():
    expo = -5.0 - np.arange(RET_HEADS, dtype=np.float64)
    return np.log1p(-np.exp2(expo))


def _ret_kernel(lg_ref, q_ref, k_ref, v_ref, g_ref, gng_ref, gnb_ref, *rest, t_len, latent):
    if latent:
        cos_ref, sin_ref, s0_ref, y_ref = rest
    else:
        y_ref, st_ref = rest
    p = pl.program_id(1)
    q = q_ref[...]
    k = k_ref[...] * (RET_DK ** -0.5)
    if latent:
        q = _rope(q, cos_ref[...], sin_ref[...], RET_DK // 4)
        k = _rope(k, cos_ref[...], sin_ref[...], RET_DK // 4)
    kb = k.astype(BF16)
    lane = lax.broadcasted_iota(jnp.int32, (t_len, LANES), 1)
    tq = ROW_TILE
    for h in range(2):
        lgf = lg_ref[2 * p + h]
        lgb = lg_ref[RET_HEADS - 1 - (2 * p + h)]
        head_lanes = (lane >= h * RET_DK) & (lane < (h + 1) * RET_DK)
        qh = jnp.where(head_lanes, q, 0.0).astype(BF16)
        vh = v_ref[:, h * RET_DV:(h + 1) * RET_DV].astype(BF16)
        for blk in range(t_len // tq):
            r0 = blk * tq
            i = lax.broadcasted_iota(jnp.int32, (tq, t_len), 0) + r0
            j = lax.broadcasted_iota(jnp.int32, (tq, t_len), 1)
            diff = (i - j).astype(F32)
            arg = jnp.where(diff > 0, lgf * diff, -lgb * diff)
            dmat = jnp.exp(arg) * jnp.where(diff == 0, 2.0, 1.0)
            qblk = qh[r0:r0 + tq]
            s = _dot_nt(qblk, kb) * dmat
            o = _dot(s.astype(BF16), vh)
            if latent:
                pos = (lax.broadcasted_iota(jnp.int32, (tq, 1), 0) + r0).astype(F32)
                o = o + _dot(qblk, s0_ref[0].astype(BF16)) * jnp.exp(lgf * (pos + 1.0))
                o = o + _dot(qblk, s0_ref[1].astype(BF16)) * jnp.exp(lgb * (t_len - pos))
            y = _ln_rows(o) * gng_ref[:, h * RET_DV:(h + 1) * RET_DV] + gnb_ref[:, h * RET_DV:(h + 1) * RET_DV]
            gate = g_ref[r0:r0 + tq, h * RET_DV:(h + 1) * RET_DV]
            y_ref[r0:r0 + tq, h * RET_DV:(h + 1) * RET_DV] = (y * _silu(gate)).astype(BF16)
        if not latent:
            pos = lax.broadcasted_iota(jnp.int32, (t_len, 1), 0).astype(F32)
            kf = (k * jnp.exp(lgf * (t_len - 1.0 - pos))).astype(BF16)
            kbw = (k * jnp.exp(lgb * pos)).astype(BF16)
            st_ref[0, h] = _dot_tn(kf, vh)[h * RET_DK:(h + 1) * RET_DK, :]
            st_ref[1, h] = _dot_tn(kbw, vh)[h * RET_DK:(h + 1) * RET_DK, :]


def _ret(proj, lg, gn_g, gn_b, *, latent, cos=None, sin=None, state=None, layer=None):
    t_len = DEC_SEQ if latent else SEQ
    nb = DEC_BATCH if latent else BATCH
    rb = N_PROMPT // DEC_SEQ if latent else 0
    pw = 2 * RET_DV
    in_specs = [
        pl.BlockSpec(memory_space=pltpu.SMEM),
        pl.BlockSpec((t_len, LANES), lambda b, p: (rb + b, COL_B_Q // LANES + p)),
        pl.BlockSpec((t_len, LANES), lambda b, p: (rb + b, COL_B_K // LANES + p)),
        pl.BlockSpec((t_len, pw), lambda b, p: (rb + b, COL_B_V // pw + p)),
        pl.BlockSpec((t_len, pw), lambda b, p: (rb + b, COL_B_G // pw + p)),
        pl.BlockSpec((1, pw), lambda b, p: (0, p)),
        pl.BlockSpec((1, pw), lambda b, p: (0, p)),
    ]
    args = [lg, proj, proj, proj, proj, gn_g, gn_b]
    out_specs = [pl.BlockSpec((t_len, pw), lambda b, p: (b, p))]
    out_shape = [jax.ShapeDtypeStruct((nb * t_len, RET_HEADS * RET_DV), BF16)]
    if latent:
        in_specs += [pl.BlockSpec((t_len, LANES), lambda b, p: (0, 0)),
                     pl.BlockSpec((t_len, LANES), lambda b, p: (0, 0)),
                     pl.BlockSpec((None, None, 2, None, 2 * RET_DK, RET_DV), lambda b, p: (b, layer, 0, p, 0, 0))]
        args += [cos, sin, state]
    else:
        out_specs.append(pl.BlockSpec((None, 2, 2, RET_DK, RET_DV), lambda b, p: (b, 0, p, 0, 0)))
        out_shape.append(jax.ShapeDtypeStruct((nb, 2, RET_HEADS, RET_DK, RET_DV), F32))
    return pl.pallas_call(
        functools.partial(_ret_kernel, t_len=t_len, latent=latent),
        grid=(nb, RET_HEADS // 2),
        in_specs=in_specs, out_specs=out_specs, out_shape=out_shape,
        compiler_params=_cparams(("arbitrary", "arbitrary"), 48),
        name="ret_latent" if latent else "ret_ctx")(*args)


def _gla_kernel(q_ref, k_ref, v_ref, g_ref, lr_ref, up_ref, ub_ref, ng_ref, *rest, t_len, latent):
    if latent:
        s0_ref, y_ref, la_scr, o_scr, st_scr = rest
    else:
        y_ref, stout_ref, la_scr, o_scr, st_scr = rest
    n_chunks = t_len // CHUNK
    lr = lr_ref[...].astype(BF16)
    for d in range(2):
        z = _dot(lr, up_ref[d].astype(BF16)) + ub_ref[d]
        la_scr[d] = (jnp.minimum(z, 0.0) - jnp.log(1.0 + jnp.exp(-jnp.abs(z)))) / GLA_NORMALIZER

    ci = lax.broadcasted_iota(jnp.int32, (CHUNK, CHUNK), 0)
    cj = lax.broadcasted_iota(jnp.int32, (CHUNK, CHUNK), 1)
    lane = lax.broadcasted_iota(jnp.int32, (CHUNK, LANES), 1)
    head_lanes = [(lane >= h * GLA_DK) & (lane < (h + 1) * GLA_DK) for h in range(2)]

    for d in range(2):
        keep = (cj <= ci) if d == 0 else (cj >= ci)
        tri = jnp.where(keep, 1.0, 0.0).astype(BF16)
        if latent:
            s0t = s0_ref[d].T
            st_scr[0] = s0t
            st_scr[1] = s0t
        else:
            st_scr[0] = jnp.zeros((GLA_DV, LANES), F32)
            st_scr[1] = jnp.zeros((GLA_DV, LANES), F32)

        def body(step, carry, d=d, keep=keep, tri=tri):
            c = step if d == 0 else n_chunks - 1 - step
            rows = pl.ds(pl.multiple_of(c * CHUNK, CHUNK), CHUNK)
            la = la_scr[d, rows, :]
            la_hi = la.astype(BF16)
            la_lo = (la - la_hi.astype(F32)).astype(BF16)
            bcum = _dot(tri, la_hi) + _dot(tri, la_lo)
            btot = bcum[CHUNK - 1:CHUNK, :] if d == 0 else bcum[0:1, :]
            q_in = q_ref[rows, :] * (GLA_DK ** -0.5) * jnp.exp(bcum)
            kk = k_ref[rows, :]
            k_in = (kk * jnp.exp(-bcum)).astype(BF16)
            k_out = (kk * jnp.exp(btot - bcum)).astype(BF16)
            dec = jnp.exp(btot)
            for h in range(2):
                qm = jnp.where(head_lanes[h], q_in, 0.0).astype(BF16)
                vh = v_ref[rows, h * GLA_DV:(h + 1) * GLA_DV].astype(BF16)
                sc = jnp.where(keep, _dot_nt(qm, k_in), 0.0).astype(BF16)
                st = st_scr[h]
                o = _dot(sc, vh) + _dot_nt(qm, st.astype(BF16))
                st_scr[h] = st * dec + _dot_tn(vh, k_out)
                if d == 0:
                    o_scr[rows, h * GLA_DV:(h + 1) * GLA_DV] = o
                else:
                    o_scr[rows, h * GLA_DV:(h + 1) * GLA_DV] += o
            return carry

        lax.fori_loop(0, n_chunks, body, 0, unroll=4)
        if not latent:
            for h in range(2):
                stout_ref[d, h] = st_scr[h].T[h * GLA_DK:(h + 1) * GLA_DK, :]

    for h in range(2):
        cols = slice(h * GLA_DV, (h + 1) * GLA_DV)
        y = _rms_rows(o_scr[:, cols]) * ng_ref[...]
        y_ref[:, cols] = (y.astype(F32) * _silu(g_ref[:, cols])).astype(BF16)


def _gla(proj, up_mat, up_bias, norm_g, *, latent, state=None, layer=None):
    t_len = DEC_SEQ if latent else SEQ
    nb = DEC_BATCH if latent else BATCH
    rb = N_PROMPT // DEC_SEQ if latent else 0
    pw = 2 * GLA_DV
    in_specs = [
        pl.BlockSpec((t_len, LANES), lambda b, p: (rb + b, COL_A_Q // LANES + p)),
        pl.BlockSpec((t_len, LANES), lambda b, p: (rb + b, COL_A_K // LANES + p)),
        pl.BlockSpec((t_len, pw), lambda b, p: (rb + b, COL_A_V // pw + p)),
        pl.BlockSpec((t_len, pw), lambda b, p: (rb + b, COL_A_G // pw + p)),
        pl.BlockSpec((t_len, LANES), lambda b, p: (rb + b, COL_A_LR // LANES)),
        pl.BlockSpec((2, LANES, LANES), lambda b, p: (0, 0, p)),
        pl.BlockSpec((2, 1, LANES), lambda b, p: (0, 0, p)),
        pl.BlockSpec((1, GLA_DV), lambda b, p: (0, 0)),
    ]
    args = [proj, proj, proj, proj, proj, up_mat, up_bias, norm_g]
    out_specs = [pl.BlockSpec((t_len, pw), lambda b, p: (b, p))]
    out_shape = [jax.ShapeDtypeStruct((nb * t_len, GLA_HEADS * GLA_DV), BF16)]
    if latent:
        in_specs.append(pl.BlockSpec((None, None, 2, None, 2 * GLA_DK, GLA_DV),
                                     lambda b, p: (b, layer, 0, p, 0, 0)))
        args.append(state)
    else:
        out_specs.append(pl.BlockSpec((None, 2, 2, GLA_DK, GLA_DV), lambda b, p: (b, 0, p, 0, 0)))
        out_shape.append(jax.ShapeDtypeStruct((nb, 2, GLA_HEADS, GLA_DK, GLA_DV), F32))
    scratch = [pltpu.VMEM((2, t_len, LANES), F32), pltpu.VMEM((t_len, pw), F32),
               pltpu.VMEM((2, GLA_DV, LANES), F32)]
    return pl.pallas_call(
        functools.partial(_gla_kernel, t_len=t_len, latent=latent),
        grid=(nb, GLA_HEADS // 2),
        in_specs=in_specs, out_specs=out_specs, out_shape=out_shape, scratch_shapes=scratch,
        compiler_params=_cparams(("arbitrary", "arbitrary"), 32),
        name="gla_latent" if latent else "gla_ctx")(*args)


def _router_kernel(h_ref, w_ref, b_ref, idx_ref, gate_ref, rank_ref, cnt_ref, run_scr):
    step = pl.program_id(0)

    @pl.when(step == 0)
    def _():
        run_scr[...] = jnp.zeros(run_scr.shape, F32)

    logits = _dot(h_ref[...], w_ref[...].astype(BF16)) + b_ref[...]
    lane = lax.broadcasted_iota(jnp.int32, logits.shape, 1)
    idx_out = jnp.zeros(logits.shape, jnp.int32)
    val_out = jnp.zeros(logits.shape, F32)
    member = jnp.zeros(logits.shape, F32)
    sels = []
    m0 = None
    denom = None
    for kk in range(TOP_K):
        m = logits.max(axis=-1, keepdims=True)
        sel = jnp.min(jnp.where(logits == m, lane, LANES), axis=-1, keepdims=True)
        sels.append(sel)
        if kk == 0:
            m0 = m
        e = jnp.exp(m - m0)
        denom = e if denom is None else denom + e
        idx_out = jnp.where(lane == kk, sel, idx_out)
        val_out = jnp.where(lane == kk, e, val_out)
        member = jnp.where(lane == sel, 1.0, member)
        logits = jnp.where(lane == sel, -jnp.inf, logits)
    idx_ref[...] = idx_out
    gate_ref[...] = val_out / denom

    ti = lax.broadcasted_iota(jnp.int32, (ROUTER_TM, ROUTER_TM), 0)
    tj = lax.broadcasted_iota(jnp.int32, (ROUTER_TM, ROUTER_TM), 1)
    earlier = jnp.where(tj < ti, 1.0, 0.0).astype(BF16)
    before = _dot(earlier, member.astype(BF16)) + run_scr[...]
    rank_out = jnp.zeros(logits.shape, F32)
    for kk in range(TOP_K):
        r = jnp.sum(jnp.where(lane == sels[kk], before, 0.0), axis=-1, keepdims=True)
        rank_out = jnp.where(lane == kk, r, rank_out)
    rank_ref[...] = rank_out.astype(jnp.int32)
    run_scr[...] = run_scr[...] + jnp.sum(member, axis=0, keepdims=True)
    cnt_ref[...] = run_scr[...]


def _router(h2, w_pad, b_pad):
    tm = ROUTER_TM
    tok_spec = pl.BlockSpec((tm, LANES), lambda i: (i, 0))
    return pl.pallas_call(
        _router_kernel,
        grid=(N_TOK // tm,),
        in_specs=[pl.BlockSpec((tm, D_MODEL), lambda i: (i, 0)),
                  pl.BlockSpec((D_MODEL, LANES), lambda i: (0, 0)),
                  pl.BlockSpec((1, LANES), lambda i: (0, 0))],
        out_specs=[tok_spec, tok_spec, tok_spec, pl.BlockSpec((1, LANES), lambda i: (0, 0))],
        out_shape=[jax.ShapeDtypeStruct((N_TOK, LANES), jnp.int32), jax.ShapeDtypeStruct((N_TOK, LANES), F32),
                   jax.ShapeDtypeStruct((N_TOK, LANES), jnp.int32), jax.ShapeDtypeStruct((1, LANES), F32)],
        scratch_shapes=[pltpu.VMEM((1, LANES), F32)],
        compiler_params=_cparams(("arbitrary",), 32),
        name="router")(h2, w_pad, b_pad)


def _tile_idx(i, nu_ref):
    return jnp.minimum(i, nu_ref[0] - 1)


def _run_start(te_ref, i):
    prev = te_ref[jnp.maximum(i - 1, 0)]
    return (i == 0) | (te_ref[i] != prev)


def _stream_weights(te_ref, nx_ref, copy_for, n_copies, on_ready):
    j = pl.program_id(0)
    i = pl.program_id(1)
    e = te_ref[i]

    @pl.when((i == 0) & (j == 0))
    def _():
        for c in range(n_copies):
            copy_for(e, j, c).start()

    for c in range(n_copies):
        copy_for(e, j, c).wait()
    on_ready()
    ne = nx_ref[e]
    jn = j + (ne <= e).astype(jnp.int32)

    @pl.when(jn < pl.num_programs(0))
    def _():
        for c in range(n_copies):
            copy_for(ne, jn, c).start()


def _moe_gu_kernel(te_ref, nu_ref, nx_ref, x_ref, w_hbm, bg_ref, bu_ref, h_ref, stage, wg_s, wu_s, sems, *, layer):
    i = pl.program_id(1)

    def copy_for(e, jj, which):
        col = pl.multiple_of(which * EXPERT_FF + jj * MOE_TF, MOE_TF)
        return pltpu.make_async_copy(w_hbm.at[layer, e, :, pl.ds(col, MOE_TF)], stage.at[which], sems.at[which])

    def cast():
        wg_s[...] = stage[0].astype(BF16)
        wu_s[...] = stage[1].astype(BF16)

    @pl.when(i < nu_ref[0])
    def _():
        @pl.when(_run_start(te_ref, i))
        def _():
            _stream_weights(te_ref, nx_ref, copy_for, 2, cast)

        x = x_ref[...]
        gate = _dot(x, wg_s[...]) + bg_ref[...]
        up = _dot(x, wu_s[...]) + bu_ref[...]
        gate = jnp.minimum(gate, SWIGLU_LIMIT)
        up = jnp.clip(up, -SWIGLU_LIMIT, SWIGLU_LIMIT)
        glu = gate * _sigmoid(SWIGLU_ALPHA * gate)
        h_ref[...] = ((up + 1.0) * glu).astype(BF16)

    @pl.when(i >= nu_ref[0])
    def _():
        h_ref[...] = jnp.zeros(h_ref.shape, BF16)


def _moe_gu(xs, w_gu, b_gu4, layer, tile_expert, n_used, next_expert):
    nj = EXPERT_FF // MOE_TF
    grid_spec = pltpu.PrefetchScalarGridSpec(
        num_scalar_prefetch=3,
        grid=(nj, MOE_TILES),
        in_specs=[
            pl.BlockSpec((MOE_TM, D_MODEL), lambda j, i, te, nu, nx: (_tile_idx(i, nu), 0)),
            pl.BlockSpec(memory_space=pl.ANY),
            pl.BlockSpec((None, None, 1, MOE_TF), lambda j, i, te, nu, nx: (layer, te[_tile_idx(i, nu)], 0, j)),
            pl.BlockSpec((None, None, 1, MOE_TF), lambda j, i, te, nu, nx: (layer, te[_tile_idx(i, nu)], 0, nj + j)),
        ],
        out_specs=pl.BlockSpec((MOE_TM, MOE_TF), lambda j, i, te, nu, nx: (i, j)),
        scratch_shapes=[pltpu.VMEM((2, D_MODEL, MOE_TF), F32), pltpu.VMEM((D_MODEL, MOE_TF), BF16),
                        pltpu.VMEM((D_MODEL, MOE_TF), BF16), pltpu.SemaphoreType.DMA((2,))])
    return pl.pallas_call(
        functools.partial(_moe_gu_kernel, layer=layer), grid_spec=grid_spec,
        out_shape=jax.ShapeDtypeStruct((MOE_SLOTS, EXPERT_FF), BF16),
        compiler_params=_cparams(("arbitrary", "arbitrary"), 44),
        name="moe_gate_up")(tile_expert, n_used, next_expert, xs, w_gu, b_gu4, b_gu4)


def _moe_down_kernel(te_ref, nu_ref, nx_ref, h_ref, w_hbm, b_ref, y_ref, stage, w_s, sems, *, layer):
    i = pl.program_id(1)

    def copy_for(e, jj, which):
        col = pl.multiple_of(jj * MOE_TN, MOE_TN)
        return pltpu.make_async_copy(w_hbm.at[layer, e, :, pl.ds(col, MOE_TN)], stage, sems.at[0])

    def cast():
        w_s[...] = stage[...].astype(BF16)

    @pl.when(i < nu_ref[0])
    def _():
        @pl.when(_run_start(te_ref, i))
        def _():
            _stream_weights(te_ref, nx_ref, copy_for, 1, cast)

        y = _dot(h_ref[...], w_s[...]) + b_ref[...]
        for c in range(Y_TILES):
            y_ref[pl.ds(c, MOE_TM, stride=Y_TILES), :] = y[:, c * LANES:(c + 1) * LANES]

    @pl.when(i >= nu_ref[0])
    def _():
        y_ref[...] = jnp.zeros(y_ref.shape, F32)


def _moe_down(hs, w_down, b_down4, layer, tile_expert, n_used, next_expert):
    grid_spec = pltpu.PrefetchScalarGridSpec(
        num_scalar_prefetch=3,
        grid=(Y_PARTS, MOE_TILES),
        in_specs=[
            pl.BlockSpec((MOE_TM, EXPERT_FF), lambda j, i, te, nu, nx: (_tile_idx(i, nu), 0)),
            pl.BlockSpec(memory_space=pl.ANY),
            pl.BlockSpec((None, None, 1, MOE_TN), lambda j, i, te, nu, nx: (layer, te[_tile_idx(i, nu)], 0, j)),
        ],
        out_specs=pl.BlockSpec((None, MOE_TM * Y_TILES, LANES), lambda j, i, te, nu, nx: (j, i, 0)),
        scratch_shapes=[pltpu.VMEM((EXPERT_FF, MOE_TN), F32), pltpu.VMEM((EXPERT_FF, MOE_TN), BF16),
                        pltpu.SemaphoreType.DMA((1,))])
    ys = pl.pallas_call(
        functools.partial(_moe_down_kernel, layer=layer), grid_spec=grid_spec,
        out_shape=jax.ShapeDtypeStruct((Y_PARTS, MOE_SLOTS * Y_TILES, LANES), F32),
        compiler_params=_cparams(("arbitrary", "arbitrary"), 44),
        name="moe_down")(tile_expert, n_used, next_expert, hs, w_down, b_down4)
    return ys.reshape(Y_PARTS, MOE_SLOTS, Y_TILES, LANES)


def _moe_plan(top_idx, rank, counts):
    experts = jnp.arange(N_EXPERTS, dtype=jnp.int32)
    padded = (counts + MOE_TM - 1) // MOE_TM * MOE_TM
    pad_end = jnp.cumsum(padded)
    pad_start = pad_end - padded
    onehot = top_idx[:, :, None] == experts[None, None, :]
    dest = jnp.sum(jnp.where(onehot, pad_start[None, None, :], 0), axis=-1).astype(jnp.int32) + rank
    tok = jnp.repeat(jnp.arange(N_TOK, dtype=jnp.int32), TOP_K)
    slot_tok = jnp.zeros((MOE_SLOTS,), jnp.int32).at[dest.reshape(-1)].set(tok, unique_indices=True)
    n_used = (pad_end[-1] // MOE_TM).astype(jnp.int32).reshape(1)
    tile_start = jnp.arange(MOE_TILES, dtype=jnp.int32) * MOE_TM
    tile_expert = jnp.minimum(jnp.sum(pad_end[None, :] <= tile_start[:, None], axis=1), N_EXPERTS - 1)
    has = counts > 0
    later = has[None, :] & (experts[None, :] > experts[:, None])
    nxt_later = jnp.min(jnp.where(later, experts[None, :], N_EXPERTS), axis=1)
    first = jnp.min(jnp.where(has, experts, N_EXPERTS))
    next_expert = jnp.where(nxt_later < N_EXPERTS, nxt_later, first).astype(jnp.int32)
    return slot_tok, dest, tile_expert.astype(jnp.int32), n_used, next_expert


def _moe(h2, h2_rows, lw, weights, layer):
    idx_pad, gate_pad, rank_pad, cnt = _router(h2, lw['router_w'], lw['router_b'])
    top_idx = idx_pad[:, :TOP_K]
    rank = rank_pad[:, :TOP_K]
    counts = cnt[0, :N_EXPERTS].astype(jnp.int32)
    slot_tok, dest, tile_expert, n_used, next_expert = _moe_plan(top_idx, rank, counts)
    xs = _gather_x(h2_rows, slot_tok)
    hs = _moe_gu(xs, weights['w_gu'], weights['b_gu'], layer, tile_expert, n_used, next_expert)
    ys = _moe_down(hs, weights['w_down'], weights['b_down'], layer, tile_expert, n_used, next_expert)
    return ys, dest, gate_pad


def _layer(x, cond16, lw, weights, layer, caches, tabs):
    state_gla, state_ret, cache_gqa_k, cache_gqa_v, cache_nat_k, cache_nat_v = caches
    mod = _adaln_mod(cond16, weights['mod_w'], weights['mod_b'], layer)
    mod3 = mod.reshape(16 * 6, 1, D_MODEL)

    h = _ln_mod(x, mod3)
    proj = _in_proj(h, lw['w_in'], tabs['zero_bias'])

    ya_c, st_a = _gla(proj, lw['gla_up'], lw['gla_ub'], lw['gla_norm_g'], latent=False)
    ya_l = _gla(proj, lw['gla_up'], lw['gla_ub'], lw['gla_norm_g'], latent=True, state=state_gla, layer=layer)[0]
    yb_c, st_b = _ret(proj, tabs['ret_lg'], lw['ret_gn_g'], lw['ret_gn_b'], latent=False)
    yb_l = _ret(proj, tabs['ret_lg'], lw['ret_gn_g'], lw['ret_gn_b'], latent=True,
                cos=tabs['cos64'], sin=tabs['sin64'], state=state_ret, layer=layer)[0]
    yc_c, kc = _ctx_attn(proj, COL_C_Q, COL_C_K, COL_C_V, GQA_KV_HEADS, GQA_HEADS // GQA_KV_HEADS,
                         lw['gqa_qn_g'], lw['gqa_kn_g'], norm=True, name="gqa_ctx")
    yc_l = _gqa_lat(proj, cache_gqa_k, cache_gqa_v, layer, lw['gqa_qn_g'], lw['gqa_kn_g'],
                    tabs['cos128'], tabs['sin128'])
    yd_c = _ctx_attn(proj, COL_D_Q, COL_D_K, COL_D_V, NAT_HEADS, 1, lw['gqa_qn_g'], lw['gqa_kn_g'],
                     norm=False, name="nat_ctx")[0]
    yd_l = _nat_lat(proj, cache_nat_k, cache_nat_v, layer, lw['nat_bias'])

    mix = _out_proj([ya_c, yb_c, yc_c, yd_c], [ya_l, yb_l, yc_l, yd_l], weights['w_o'], layer)
    x1, h2, h2_rows = _res_ln1(x, mix, mod3, lw['ln1_g'], lw['ln1_b'])
    y_rows, dest, gate_pad = _moe(h2, h2_rows, lw, weights, layer)
    x2 = _res_ln2(x1, y_rows, dest, gate_pad, mod3, lw['ln2_g'], lw['ln2_b'])

    vc = proj[:N_PROMPT, COL_C_V:COL_C_V + GQA_KV_HEADS * GQA_DIM]
    kd = proj[:N_PROMPT, COL_D_K:COL_D_K + NAT_HEADS * NAT_DIM]
    vd = proj[:N_PROMPT, COL_D_V:COL_D_V + NAT_HEADS * NAT_DIM]
    new = (st_a, st_b,
           kc.reshape(BATCH, SEQ, GQA_KV_HEADS, GQA_DIM), vc.reshape(BATCH, SEQ, GQA_KV_HEADS, GQA_DIM),
           kd.reshape(BATCH, SEQ, NAT_HEADS, NAT_DIM), vd.reshape(BATCH, SEQ, NAT_HEADS, NAT_DIM))
    return x2, new


def _prep_layer_weights(l, w_in, gla_lr_up, gla_lr_b, gla_norm_g, ret_gn_g, ret_gn_b, gqa_qn_g, gqa_kn_g, nat_rpb,
                        ln1_g, ln1_b, router_w, router_b, ln2_g, ln2_b):
    wi = w_in[l]
    w_in_p = jnp.concatenate([wi[:, :ALR_SRC], wi[:, ALR_SRC + ALR_W:], wi[:, ALR_SRC:ALR_SRC + ALR_W],
                              jnp.zeros((D_MODEL, PROJ_PAD - wi.shape[1]), wi.dtype)], axis=1).astype(BF16)
    up = jnp.zeros((2, LANES, GLA_HEADS * GLA_DK), F32)
    for d in range(2):
        up = up.at[d, d * GLA_RANK:(d + 1) * GLA_RANK, :].set(gla_lr_up[l, d])
    rw = jnp.zeros((D_MODEL, LANES), F32).at[:, :N_EXPERTS].set(router_w[l])
    rb = jnp.full((1, LANES), NEG_BIG, F32).at[0, :N_EXPERTS].set(router_b[l])
    return {
        'w_in': w_in_p,
        'gla_up': up, 'gla_ub': gla_lr_b[l].reshape(2, 1, -1), 'gla_norm_g': gla_norm_g[l].reshape(1, -1),
        'ret_gn_g': ret_gn_g[l].reshape(1, -1), 'ret_gn_b': ret_gn_b[l].reshape(1, -1),
        'gqa_qn_g': gqa_qn_g[l].reshape(1, -1), 'gqa_kn_g': gqa_kn_g[l].reshape(1, -1),
        'nat_bias': _nat_bias_table(nat_rpb[l]),
        'ln1_g': ln1_g[l].reshape(1, -1), 'ln1_b': ln1_b[l].reshape(1, -1),
        'router_w': rw, 'router_b': rb,
        'ln2_g': ln2_g[l].reshape(1, -1), 'ln2_b': ln2_b[l].reshape(1, -1),
    }


def kernel(x_prompt, x_sample, state_gla, state_ret, cache_gqa_k, cache_gqa_v, cache_nat_k, cache_nat_v, c, c_ctx, mod_w, mod_b, w_in, gla_lr_up, gla_lr_b, gla_norm_g, ret_gn_g, ret_gn_b, gqa_qn_g, gqa_kn_g, nat_rpb, w_o, ln1_g, ln1_b, router_w, router_b, w_gu, b_gu, w_down, b_down, ln2_g, ln2_b):
    x = jnp.concatenate([x_prompt.reshape(N_PROMPT, D_MODEL), x_sample.reshape(N_SAMPLE, D_MODEL)], axis=0)
    cond16 = jnp.zeros((16, D_MODEL), F32).at[0].set(c_ctx).at[1:1 + DEC_BATCH].set(c)
    cos128, sin128 = _rope_tables(DEC_SEQ, GQA_DIM // 4)
    cos64, sin64 = _rope_tables(DEC_SEQ, RET_DK // 4)
    tabs = {'cos128': cos128, 'sin128': sin128, 'cos64': cos64, 'sin64': sin64,
            'ret_lg': jnp.asarray(_ret_log_decays(), F32), 'zero_bias': jnp.zeros((1, PROJ_PAD), F32)}
    weights = {'mod_w': mod_w, 'mod_b': mod_b.reshape(DEPTH, 1, -1), 'w_o': w_o,
               'w_gu': w_gu, 'b_gu': b_gu.reshape(DEPTH, N_EXPERTS, 1, -1),
               'w_down': w_down, 'b_down': b_down.reshape(DEPTH, N_EXPERTS, 1, -1)}
    caches = (state_gla.reshape(DEC_BATCH, DEPTH, 2, GLA_HEADS // 2, 2 * GLA_DK, GLA_DV),
              state_ret.reshape(DEC_BATCH, DEPTH, 2, RET_HEADS // 2, 2 * RET_DK, RET_DV),
              cache_gqa_k.reshape(DEC_BATCH, DEPTH, PAST_LEN, GQA_KV_HEADS * GQA_DIM),
              cache_gqa_v.reshape(DEC_BATCH, DEPTH, PAST_LEN, GQA_KV_HEADS * GQA_DIM),
              cache_nat_k.reshape(DEC_BATCH, DEPTH, PAST_LEN, NAT_HEADS * NAT_DIM),
              cache_nat_v.reshape(DEC_BATCH, DEPTH, PAST_LEN, NAT_HEADS * NAT_DIM))
    news = []
    for l in range(DEPTH):
        lw = _prep_layer_weights(l, w_in, gla_lr_up, gla_lr_b, gla_norm_g, ret_gn_g, ret_gn_b, gqa_qn_g, gqa_kn_g,
                                 nat_rpb, ln1_g, ln1_b, router_w, router_b, ln2_g, ln2_b)
        x, new = _layer(x, cond16, lw, weights, l, caches, tabs)
        news.append(new)
    y_p = x[:N_PROMPT].reshape(BATCH, SEQ, D_MODEL)
    y_s = x[N_PROMPT:].reshape(DEC_BATCH, DEC_SEQ, D_MODEL)
    stacked = tuple(jnp.stack([news[l][k] for l in range(DEPTH)], axis=1) for k in range(6))
    return (y_p, y_s) + stacked
```

```python
import functools

import numpy as np
import jax
import jax.numpy as jnp
from jax import lax
from jax.experimental import pallas as pl
from jax.experimental.pallas import tpu as pltpu

F32 = jnp.float32
BF16 = jnp.bfloat16
U32 = jnp.uint32
I32 = jnp.int32

D_MODEL = 4096
BATCH = 16
SEQ = 256
DEPTH = 2
DEC_BATCH = 8
DEC_SEQ = 1024
PAST_LEN = 512
GRID_W = 64
GLA_HEADS = 8
GLA_DK = 64
GLA_DV = 128
GLA_RANK = 16
GLA_NORMALIZER = 16.0
RET_HEADS = 8
RET_DK = 64
RET_DV = 128
GQA_HEADS = 8
GQA_KV_HEADS = 2
GQA_DIM = 128
NAT_HEADS = 8
NAT_DIM = 128
NAT_ROWS = 8
NAT_COLS = 16
CHUNK = 64
ROPE_THETA = 10000.0
LN_EPS = 1e-6
N_EXPERTS = 32
TOP_K = 4
EXPERT_FF = 2048
SWIGLU_LIMIT = 7.0
SWIGLU_ALPHA = 1.702
DEEPNORM_ALPHA = (2.0 * DEPTH) ** 0.25

N_PROMPT = BATCH * SEQ
N_SAMPLE = DEC_BATCH * DEC_SEQ
N_TOK = N_PROMPT + N_SAMPLE

LANES = 128
SUBLANES = 8
ROW_TILE = 256
NEG_BIG = -1e30

COL_A_Q, COL_A_K, COL_A_V, COL_A_G = 0, 512, 1024, 2048
COL_B_Q, COL_B_K, COL_B_V, COL_B_G = 3072, 3584, 4096, 5120
COL_C_Q, COL_C_K, COL_C_V = 6144, 7168, 7424
COL_D_Q, COL_D_K, COL_D_V = 7680, 8704, 9728
COL_A_LR = 10752
PROJ_PAD = 11264
ALR_SRC = 3072
ALR_W = 2 * GLA_RANK

MOE_TM = 256
NK = N_TOK * TOP_K
MOE_TILES = NK // MOE_TM + N_EXPERTS
MOE_SLOTS = MOE_TILES * MOE_TM
MOE_TF = 512
MOE_TN = 2048
ROUTER_TM = 512

HALF = D_MODEL // 2
X_TILES = HALF // LANES
Y_PARTS = D_MODEL // MOE_TN
Y_TILES = MOE_TN // LANES
X_PITCH = X_TILES + SUBLANES
Y_PITCH = Y_TILES + SUBLANES


def _cparams(sem, vmem_mb):
    return pltpu.CompilerParams(dimension_semantics=sem, vmem_limit_bytes=int(vmem_mb * 1024 * 1024))


def _dot(a, b):
    return jnp.dot(a, b, preferred_element_type=F32)


def _dot_nt(a, b):
    return lax.dot_general(a, b, (((1,), (1,)), ((), ())), preferred_element_type=F32)


def _dot_tn(a, b):
    return lax.dot_general(a, b, (((0,), (0,)), ((), ())), preferred_element_type=F32)


def _sigmoid(x):
    return 1.0 / (1.0 + jnp.exp(-x))


def _silu(x):
    return x * _sigmoid(x)


def _ln_rows(x):
    mu = jnp.mean(x, axis=-1, keepdims=True)
    xc = x - mu
    var = jnp.mean(xc * xc, axis=-1, keepdims=True)
    return xc * lax.rsqrt(var + LN_EPS)


def _rms_rows(x):
    return x * lax.rsqrt(jnp.mean(x * x, axis=-1, keepdims=True) + LN_EPS)


def _pack_words(a, b):
    ab = pltpu.bitcast(a.astype(BF16).astype(F32), U32)
    bb = pltpu.bitcast(b.astype(BF16).astype(F32), U32)
    return (ab >> 16) | (bb & jnp.uint32(0xFFFF0000))


def _unpack_words(w):
    lo = pltpu.bitcast(w << 16, F32)
    hi = pltpu.bitcast(w & jnp.uint32(0xFFFF0000), F32)
    return lo, hi


def _mod_spec(which, tile):
    n_ctx_tiles = N_PROMPT // tile
    per_b = DEC_SEQ // tile

    def imap(i, *_):
        row = jnp.where(i < n_ctx_tiles, 0, 1 + (i - n_ctx_tiles) // per_b)
        return (row * 6 + which, 0, 0)
    return pl.BlockSpec((1, 1, D_MODEL), imap)


def _mm_kernel(a_ref, w_ref, b_ref, o_ref, *, silu_a):
    a = a_ref[...]
    if silu_a:
        a = _silu(a.astype(F32))
    acc = _dot(a.astype(BF16), w_ref[...].astype(BF16))
    o_ref[...] = acc + b_ref[...]


def _adaln_mod(cond16, mod_w, mod_b3, layer):
    k, n = mod_w.shape[1], mod_w.shape[2]
    tn = 512
    return pl.pallas_call(
        functools.partial(_mm_kernel, silu_a=True),
        grid=(1, n // tn),
        in_specs=[pl.BlockSpec((16, k), lambda i, j: (0, 0)),
                  pl.BlockSpec((None, k, tn), lambda i, j: (layer, 0, j)),
                  pl.BlockSpec((None, 1, tn), lambda i, j: (layer, 0, j))],
        out_specs=pl.BlockSpec((16, tn), lambda i, j: (0, j)),
        out_shape=jax.ShapeDtypeStruct((16, n), F32),
        compiler_params=_cparams(("arbitrary", "arbitrary"), 40),
        name="adaln_mod")(cond16, mod_w, mod_b3)


def _mm_plain_kernel(a_ref, w_ref, o_ref):
    o_ref[...] = _dot(a_ref[...], w_ref[...])


def _in_proj(h, w_in_p):
    tm, tn = 1024, 512
    k = h.shape[1]
    return pl.pallas_call(
        _mm_plain_kernel,
        grid=(N_TOK // tm, PROJ_PAD // tn),
        in_specs=[pl.BlockSpec((tm, k), lambda i, j: (i, 0)),
                  pl.BlockSpec((k, tn), lambda i, j: (0, j))],
        out_specs=pl.BlockSpec((tm, tn), lambda i, j: (i, j)),
        out_shape=jax.ShapeDtypeStruct((N_TOK, PROJ_PAD), F32),
        compiler_params=_cparams(("arbitrary", "arbitrary"), 40),
        name="in_proj")(h, w_in_p)


def _mm4_kernel(*refs, n_ctx_tiles):
    ctx, lat, ws, o_ref = refs[0:4], refs[4:8], refs[8:12], refs[12]

    def project(parts):
        acc = _dot(parts[0][...], ws[0][...].astype(BF16))
        for a, w in zip(parts[1:], ws[1:]):
            acc += _dot(a[...], w[...].astype(BF16))
        o_ref[...] = acc

    is_ctx = pl.program_id(0) < n_ctx_tiles
    pl.when(is_ctx)(lambda: project(ctx))
    pl.when(jnp.logical_not(is_ctx))(lambda: project(lat))


def _out_proj(ys_ctx, ys_lat, w_o, layer, *, tm=1024, tn=256):
    kp = ys_ctx[0].shape[1]
    n_ctx_tiles = N_PROMPT // tm
    ctx_specs = [pl.BlockSpec((tm, kp), lambda i, j: (jnp.minimum(i, n_ctx_tiles - 1), 0)) for _ in range(4)]
    lat_specs = [pl.BlockSpec((tm, kp), lambda i, j: (jnp.maximum(i - n_ctx_tiles, 0), 0)) for _ in range(4)]
    w_specs = [pl.BlockSpec((None, kp, tn), functools.partial(lambda i, j, p: (layer, p, j), p=p)) for p in range(4)]
    return pl.pallas_call(
        functools.partial(_mm4_kernel, n_ctx_tiles=n_ctx_tiles),
        grid=(N_TOK // tm, D_MODEL // tn),
        in_specs=ctx_specs + lat_specs + w_specs,
        out_specs=pl.BlockSpec((tm, tn), lambda i, j: (i, j)),
        out_shape=jax.ShapeDtypeStruct((N_TOK, D_MODEL), F32),
        compiler_params=_cparams(("arbitrary", "arbitrary"), 48),
        name="out_proj")(*ys_ctx, *ys_lat, w_o, w_o, w_o, w_o)


def _ln_mod_kernel(x_ref, sh_ref, sc_ref, h_ref):
    h = _ln_rows(x_ref[...]) * (1.0 + sc_ref[0]) + sh_ref[0]
    h_ref[...] = h.astype(BF16)


def _ln_mod(x, mod3):
    return pl.pallas_call(
        _ln_mod_kernel,
        grid=(N_TOK // ROW_TILE,),
        in_specs=[pl.BlockSpec((ROW_TILE, D_MODEL), lambda i: (i, 0)), _mod_spec(0, ROW_TILE),
                  _mod_spec(1, ROW_TILE)],
        out_specs=pl.BlockSpec((ROW_TILE, D_MODEL), lambda i: (i, 0)),
        out_shape=jax.ShapeDtypeStruct((N_TOK, D_MODEL), BF16),
        compiler_params=_cparams(("arbitrary",), 32),
        name="ln_mod")(x, mod3, mod3)


def _res_ln1_kernel(x_ref, mix_ref, gate_ref, g_ref, b_ref, sh_ref, sc_ref, x1_ref, h2_ref, rows_ref):
    y = DEEPNORM_ALPHA * x_ref[...] + gate_ref[0] * mix_ref[...]
    x1 = _ln_rows(y) * g_ref[...] + b_ref[...]
    x1_ref[...] = x1
    h2 = _ln_rows(x1) * (1.0 + sc_ref[0]) + sh_ref[0]
    h2_ref[...] = h2.astype(BF16)
    for c in range(X_TILES):
        lo = h2[:, c * LANES:(c + 1) * LANES]
        hi = h2[:, HALF + c * LANES:HALF + (c + 1) * LANES]
        rows_ref[pl.ds(c, ROW_TILE, stride=X_TILES), :] = _pack_words(lo, hi)


def _res_ln1(x, mix, mod3, ln_g, ln_b):
    tile = ROW_TILE
    row_spec = pl.BlockSpec((tile, D_MODEL), lambda i: (i, 0))
    vec_spec = pl.BlockSpec((1, D_MODEL), lambda i: (0, 0))
    x1, h2, rows = pl.pallas_call(
        _res_ln1_kernel,
        grid=(N_TOK // tile,),
        in_specs=[row_spec, row_spec, _mod_spec(2, tile), vec_spec, vec_spec, _mod_spec(3, tile), _mod_spec(4, tile)],
        out_specs=[row_spec, row_spec, pl.BlockSpec((tile * X_TILES, LANES), lambda i: (i, 0))],
        out_shape=[jax.ShapeDtypeStruct((N_TOK, D_MODEL), F32), jax.ShapeDtypeStruct((N_TOK, D_MODEL), BF16),
                   jax.ShapeDtypeStruct((N_TOK * X_TILES, LANES), U32)],
        compiler_params=_cparams(("arbitrary",), 48),
        name="res_ln_mixer")(x, mix, mod3, ln_g, ln_b, mod3, mod3)
    return x1, h2, rows.reshape(1, N_TOK, X_TILES, LANES)


def _gather_step(idx_cur, idx_nxt, src_ref, buf, sems, n_rows, tiles, pitch):
    step = pl.program_id(0)
    slot = step % 2

    def issue(idx_ref, to_slot):
        def body(pair, carry):
            for queue in range(2):
                r = pair * 2 + queue
                dst = buf.at[to_slot, :, pl.ds(pl.multiple_of(r * pitch, SUBLANES), tiles), :]
                pltpu.make_async_copy(src_ref.at[:, idx_ref[0, 0, r]], dst, sems.at[to_slot]).start(priority=queue)
            return carry
        lax.fori_loop(0, n_rows // 2, body, 0, unroll=4)

    @pl.when(step == 0)
    def _():
        issue(idx_cur, slot)

    @pl.when(step + 1 < pl.num_programs(0))
    def _():
        issue(idx_nxt, 1 - slot)

    landed = buf.at[slot, :, pl.ds(0, n_rows * tiles), :]
    pltpu.make_async_copy(landed, landed, sems.at[slot]).wait()
    return slot


def _idx_specs(n_rows, n_steps):
    return [pl.BlockSpec((1, 1, n_rows), lambda i: (i, 0, 0), memory_space=pltpu.SMEM),
            pl.BlockSpec((1, 1, n_rows), lambda i: (jnp.minimum(i + 1, n_steps - 1), 0, 0),
                         memory_space=pltpu.SMEM)]


def _res_ln2_kernel(idx_cur, idx_nxt, x_ref, y_hbm, gates_ref, gate_ref, g_ref, b_ref, o_ref, buf, sems, mix_scr,
                    *, tile):
    slot = _gather_step(idx_cur, idx_nxt, y_hbm, buf, sems, TOP_K * tile, Y_TILES, Y_PITCH)
    gates = gates_ref[...]
    for part in range(Y_PARTS):
        for c in range(Y_TILES):
            acc = None
            for kk in range(TOP_K):
                slab = buf[slot, part, pl.ds(kk * tile * Y_PITCH + c, tile, stride=Y_PITCH), :]
                term = gates[:, kk:kk + 1] * slab
                acc = term if acc is None else acc + term
            col = part * MOE_TN + c * LANES
            mix_scr[:, col:col + LANES] = acc
    y = DEEPNORM_ALPHA * x_ref[...] + gate_ref[0] * mix_scr[...]
    o_ref[...] = _ln_rows(y) * g_ref[...] + b_ref[...]


def _res_ln2(x1, y_rows, dest, gates_pad, mod3, ln_g, ln_b):
    tile = 64
    n_tiles = N_TOK // tile
    idx = dest.reshape(n_tiles, tile, TOP_K).transpose(0, 2, 1).reshape(n_tiles, 1, TOP_K * tile)
    row_spec = pl.BlockSpec((tile, D_MODEL), lambda i: (i, 0))
    vec_spec = pl.BlockSpec((1, D_MODEL), lambda i: (0, 0))
    return pl.pallas_call(
        functools.partial(_res_ln2_kernel, tile=tile),
        grid=(n_tiles,),
        in_specs=_idx_specs(TOP_K * tile, n_tiles) + [row_spec, pl.BlockSpec(memory_space=pl.ANY),
                                                      pl.BlockSpec((tile, LANES), lambda i: (i, 0)),
                                                      _mod_spec(5, tile), vec_spec, vec_spec],
        out_specs=row_spec,
        out_shape=jax.ShapeDtypeStruct((N_TOK, D_MODEL), F32),
        scratch_shapes=[pltpu.VMEM((2, Y_PARTS, TOP_K * tile * Y_PITCH, LANES), F32), pltpu.SemaphoreType.DMA((2,)),
                        pltpu.VMEM((tile, D_MODEL), F32)],
        compiler_params=_cparams(("arbitrary",), 40),
        name="res_ln_moe")(idx, idx, x1, y_rows, gates_pad, mod3, ln_g, ln_b)


def _gather_x_kernel(idx_cur, idx_nxt, src_hbm, o_ref, buf, sems):
    slot = _gather_step(idx_cur, idx_nxt, src_hbm, buf, sems, MOE_TM, X_TILES, X_PITCH)
    for c in range(X_TILES):
        lo, hi = _unpack_words(buf[slot, 0, pl.ds(c, MOE_TM, stride=X_PITCH), :])
        o_ref[:, c * LANES:(c + 1) * LANES] = lo.astype(BF16)
        o_ref[:, HALF + c * LANES:HALF + (c + 1) * LANES] = hi.astype(BF16)


def _gather_x(h2_rows, slot_tok):
    idx = slot_tok.reshape(MOE_TILES, 1, MOE_TM)
    return pl.pallas_call(
        _gather_x_kernel,
        grid=(MOE_TILES,),
        in_specs=_idx_specs(MOE_TM, MOE_TILES) + [pl.BlockSpec(memory_space=pl.ANY)],
        out_specs=pl.BlockSpec((MOE_TM, D_MODEL), lambda i: (i, 0)),
        out_shape=jax.ShapeDtypeStruct((MOE_SLOTS, D_MODEL), BF16),
        scratch_shapes=[pltpu.VMEM((2, 1, MOE_TM * X_PITCH, LANES), U32), pltpu.SemaphoreType.DMA((2,))],
        compiler_params=_cparams(("arbitrary",), 32),
        name="moe_gather_x")(idx, idx, h2_rows)


def _rope(x, cos, sin_signed, quarter):
    lane = lax.broadcasted_iota(jnp.int32, x.shape, 1)
    up = pltpu.roll(x, LANES - quarter, axis=1)
    dn = pltpu.roll(x, quarter, axis=1)
    partner = jnp.where((lane % (2 * quarter)) < quarter, up, dn)
    return x * cos + partner * sin_signed


def _rope_tables(t_len, quarter):
    lane = np.arange(LANES)
    within = lane % (4 * quarter)
    is_col = within >= 2 * quarter
    j = within % quarter
    second = (within % (2 * quarter)) >= quarter
    freqs = ROPE_THETA ** (-(j.astype(np.float32)) / quarter)
    t = np.arange(t_len)
    pos = np.where(is_col[None, :], (t % GRID_W)[:, None], (t // GRID_W)[:, None]).astype(np.float32)
    ang = pos * freqs[None, :].astype(np.float32)
    cos = np.cos(ang).astype(np.float32)
    sin = np.sin(ang).astype(np.float32) * np.where(second, 1.0, -1.0)[None, :].astype(np.float32)
    return jnp.asarray(cos), jnp.asarray(sin)


def _attend(q, keys, vals, scale, biases=None):
    ss = []
    for idx, k in enumerate(keys):
        s = _dot_nt(q, k) * scale
        if biases is not None and biases[idx] is not None:
            s = s + biases[idx]
        ss.append(s)
    m = ss[0].max(axis=-1, keepdims=True)
    for s in ss[1:]:
        m = jnp.maximum(m, s.max(axis=-1, keepdims=True))
    l = None
    o = None
    for s, v in zip(ss, vals):
        p = jnp.exp(s - m)
        pl_sum = p.sum(axis=-1, keepdims=True)
        po = _dot(p.astype(BF16), v)
        l = pl_sum if l is None else l + pl_sum
        o = po if o is None else o + po
    return o / l


def _ctx_attn_kernel(q_ref, k_ref, v_ref, gq_ref, gk_ref, o_ref, *rest, groups, norm):
    k = k_ref[...]
    if norm:
        k = _rms_rows(k) * gk_ref[...]
        rest[0][...] = k
    kb = k.astype(BF16)
    vb = v_ref[...].astype(BF16)
    scale = GQA_DIM ** -0.5
    for g in range(groups):
        q = q_ref[:, g * LANES:(g + 1) * LANES]
        if norm:
            q = _rms_rows(q) * gq_ref[...]
        o_ref[:, g * LANES:(g + 1) * LANES] = _attend(q.astype(BF16), [kb], [vb], scale).astype(BF16)


def _ctx_attn(proj, col_q, col_k, col_v, n_kv, groups, gq, gk, *, norm, name):
    qw = groups * LANES
    in_specs = [pl.BlockSpec((SEQ, qw), lambda b, h: (b, col_q // qw + h)),
                pl.BlockSpec((SEQ, LANES), lambda b, h: (b, col_k // LANES + h)),
                pl.BlockSpec((SEQ, LANES), lambda b, h: (b, col_v // LANES + h)),
                pl.BlockSpec((1, LANES), lambda b, h: (0, 0)),
                pl.BlockSpec((1, LANES), lambda b, h: (0, 0))]
    out_specs = [pl.BlockSpec((SEQ, qw), lambda b, h: (b, h))]
    out_shape = [jax.ShapeDtypeStruct((N_PROMPT, n_kv * qw), BF16)]
    if norm:
        out_specs.append(pl.BlockSpec((SEQ, LANES), lambda b, h: (b, h)))
        out_shape.append(jax.ShapeDtypeStruct((N_PROMPT, n_kv * LANES), F32))
    return pl.pallas_call(
        functools.partial(_ctx_attn_kernel, groups=groups, norm=norm),
        grid=(BATCH, n_kv),
        in_specs=in_specs, out_specs=out_specs, out_shape=out_shape,
        compiler_params=_cparams(("arbitrary", "arbitrary"), 32),
        name=name)(proj, proj, proj, gq, gk)


def _gqa_lat_kernel(q_ref, k_ref, v_ref, ck_ref, cv_ref, gq_ref, gk_ref, cosq_ref, sinq_ref, cosk_ref, sink_ref,
                    o_ref, k_scr, v_scr):
    qb = pl.program_id(2)

    @pl.when(qb == 0)
    def _():
        k = _rms_rows(k_ref[...]) * gk_ref[...]
        k_scr[...] = _rope(k, cosk_ref[...], sink_ref[...], GQA_DIM // 4).astype(BF16)
        v_scr[...] = v_ref[...].astype(BF16)

    ck = ck_ref[...].astype(BF16)
    cv = cv_ref[...].astype(BF16)
    scale = GQA_DIM ** -0.5
    for g in range(GQA_HEADS // GQA_KV_HEADS):
        q = _rms_rows(q_ref[:, g * LANES:(g + 1) * LANES]) * gq_ref[...]
        q = _rope(q, cosq_ref[...], sinq_ref[...], GQA_DIM // 4).astype(BF16)
        o = _attend(q, [k_scr[...], ck], [v_scr[...], cv], scale)
        o_ref[:, g * LANES:(g + 1) * LANES] = o.astype(BF16)


def _gqa_lat(proj, cache_k, cache_v, layer, gq, gk, cos, sin):
    groups = GQA_HEADS // GQA_KV_HEADS
    qw = groups * LANES
    tq = ROW_TILE
    nqb = DEC_SEQ // tq
    row0 = N_PROMPT // tq
    in_specs = [
        pl.BlockSpec((tq, qw), lambda b, h, i: (row0 + b * nqb + i, COL_C_Q // qw + h)),
        pl.BlockSpec((DEC_SEQ, LANES), lambda b, h, i: (N_PROMPT // DEC_SEQ + b, COL_C_K // LANES + h)),
        pl.BlockSpec((DEC_SEQ, LANES), lambda b, h, i: (N_PROMPT // DEC_SEQ + b, COL_C_V // LANES + h)),
        pl.BlockSpec((None, None, PAST_LEN, LANES), lambda b, h, i: (b, layer, 0, h)),
        pl.BlockSpec((None, None, PAST_LEN, LANES), lambda b, h, i: (b, layer, 0, h)),
        pl.BlockSpec((1, LANES), lambda b, h, i: (0, 0)),
        pl.BlockSpec((1, LANES), lambda b, h, i: (0, 0)),
        pl.BlockSpec((tq, LANES), lambda b, h, i: (i, 0)),
        pl.BlockSpec((tq, LANES), lambda b, h, i: (i, 0)),
        pl.BlockSpec((DEC_SEQ, LANES), lambda b, h, i: (0, 0)),
        pl.BlockSpec((DEC_SEQ, LANES), lambda b, h, i: (0, 0)),
    ]
    return pl.pallas_call(
        _gqa_lat_kernel,
        grid=(DEC_BATCH, GQA_KV_HEADS, nqb),
        in_specs=in_specs,
        out_specs=pl.BlockSpec((tq, qw), lambda b, h, i: (b * nqb + i, h)),
        out_shape=jax.ShapeDtypeStruct((N_SAMPLE, GQA_HEADS * GQA_DIM), BF16),
        scratch_shapes=[pltpu.VMEM((DEC_SEQ, LANES), BF16), pltpu.VMEM((DEC_SEQ, LANES), BF16)],
        compiler_params=_cparams(("arbitrary", "arbitrary", "arbitrary"), 40),
        name="gqa_latent")(proj, proj, proj, cache_k, cache_v, gq, gk, cos, sin, cos, sin)


NAT_GRID_ROWS = DEC_SEQ // GRID_W
NAT_KR = min(NAT_ROWS, NAT_GRID_ROWS)
NAT_BAND = NAT_KR * GRID_W
NAT_PATTERNS = NAT_KR


def _nat_lat_kernel(q_ref, k_ref, v_ref, ck_ref, cv_ref, bias_ref, o_ref, k_scr, v_scr, ck_scr, cv_scr):
    k_scr[...] = k_ref[...].astype(BF16)
    v_scr[...] = v_ref[...].astype(BF16)
    ck_scr[...] = ck_ref[...].astype(BF16)
    cv_scr[...] = cv_ref[...].astype(BF16)
    scale = NAT_DIM ** -0.5

    def body(r, carry):
        band = jnp.clip(r - NAT_KR // 2, 0, NAT_GRID_ROWS - NAT_KR)
        start = pl.multiple_of(band * GRID_W, GRID_W)
        rows = pl.ds(pl.multiple_of(r * GRID_W, GRID_W), GRID_W)
        q = q_ref[rows, :].astype(BF16)
        kb = k_scr[pl.ds(start, NAT_BAND), :]
        vb = v_scr[pl.ds(start, NAT_BAND), :]
        bias = bias_ref[r - band]
        o = _attend(q, [kb, ck_scr[...]], [vb, cv_scr[...]], scale, biases=[bias, None])
        o_ref[rows, :] = o.astype(BF16)
        return carry

    lax.fori_loop(0, NAT_GRID_ROWS, body, 0, unroll=8)


def _nat_lat(proj, cache_k, cache_v, layer, bias_tab):
    rb = N_PROMPT // DEC_SEQ
    in_specs = [
        pl.BlockSpec((DEC_SEQ, LANES), lambda b, h: (rb + b, COL_D_Q // LANES + h)),
        pl.BlockSpec((DEC_SEQ, LANES), lambda b, h: (rb + b, COL_D_K // LANES + h)),
        pl.BlockSpec((DEC_SEQ, LANES), lambda b, h: (rb + b, COL_D_V // LANES + h)),
        pl.BlockSpec((None, None, PAST_LEN, LANES), lambda b, h: (b, layer, 0, h)),
        pl.BlockSpec((None, None, PAST_LEN, LANES), lambda b, h: (b, layer, 0, h)),
        pl.BlockSpec((None, NAT_PATTERNS, GRID_W, NAT_BAND), lambda b, h: (h, 0, 0, 0)),
    ]
    return pl.pallas_call(
        _nat_lat_kernel,
        grid=(DEC_BATCH, NAT_HEADS),
        in_specs=in_specs,
        out_specs=pl.BlockSpec((DEC_SEQ, LANES), lambda b, h: (b, h)),
        out_shape=jax.ShapeDtypeStruct((N_SAMPLE, NAT_HEADS * NAT_DIM), BF16),
        scratch_shapes=[pltpu.VMEM((DEC_SEQ, LANES), BF16), pltpu.VMEM((DEC_SEQ, LANES), BF16),
                        pltpu.VMEM((PAST_LEN, LANES), BF16), pltpu.VMEM((PAST_LEN, LANES), BF16)],
        compiler_params=_cparams(("arbitrary", "arbitrary"), 32),
        name="nat_latent")(proj, proj, proj, cache_k, cache_v, bias_tab)


def _nat_bias_table(rpb):
    off = np.arange(NAT_PATTERNS)
    kr = np.arange(NAT_KR)
    rel_r = kr[None, :] - off[:, None] + NAT_ROWS - 1
    c = np.arange(GRID_W)
    rel_c = np.clip(c[None, :] - c[:, None], 1 - NAT_COLS, NAT_COLS - 1) + NAT_COLS - 1
    col_start = np.clip(c - NAT_COLS // 2, 0, GRID_W - NAT_COLS)
    col_ok = (c[None, :] >= col_start[:, None]) & (c[None, :] < col_start[:, None] + NAT_COLS)
    n_pk = NAT_PATTERNS * NAT_KR
    row_sel = np.zeros((n_pk, 2 * NAT_ROWS - 1), np.float32)
    row_sel[np.arange(n_pk), rel_r.reshape(-1)] = 1.0
    col_sel = np.zeros((2 * NAT_COLS - 1, GRID_W * GRID_W), np.float32)
    col_sel[rel_c.reshape(-1), np.arange(GRID_W * GRID_W)] = 1.0
    bias = jnp.einsum('ka,hab,bx->hkx', jnp.asarray(row_sel), rpb.astype(F32), jnp.asarray(col_sel),
                      precision=lax.Precision.HIGHEST)
    bias = bias.reshape(NAT_HEADS, NAT_PATTERNS, NAT_KR, GRID_W, GRID_W).transpose(0, 1, 3, 2, 4)
    bias = jnp.where(jnp.asarray(col_ok)[None, None, :, None, :], bias, NEG_BIG)
    return bias.reshape(NAT_HEADS, NAT_PATTERNS, GRID_W, NAT_BAND)


def _ret_log_decays():
    expo = -5.0 - np.arange(RET_HEADS, dtype=np.float64)
    return np.log1p(-np.exp2(expo))


def _ret_kernel(lg_ref, q_ref, k_ref, v_ref, g_ref, gng_ref, gnb_ref, *rest, t_len, latent):
    if latent:
        cos_ref, sin_ref, s0_ref, y_ref = rest
    else:
        y_ref, st_ref = rest
    p = pl.program_id(1)
    q = q_ref[...]
    k = k_ref[...] * (RET_DK ** -0.5)
    if latent:
        q = _rope(q, cos_ref[...], sin_ref[...], RET_DK // 4)
        k = _rope(k, cos_ref[...], sin_ref[...], RET_DK // 4)
    kb = k.astype(BF16)
    lane = lax.broadcasted_iota(jnp.int32, (t_len, LANES), 1)
    tq = ROW_TILE
    for h in range(2):
        lgf = lg_ref[2 * p + h]
        lgb = lg_ref[RET_HEADS - 1 - (2 * p + h)]
        head_lanes = (lane >= h * RET_DK) & (lane < (h + 1) * RET_DK)
        qh = jnp.where(head_lanes, q, 0.0).astype(BF16)
        vh = v_ref[:, h * RET_DV:(h + 1) * RET_DV].astype(BF16)
        for blk in range(t_len // tq):
            r0 = blk * tq
            i = lax.broadcasted_iota(jnp.int32, (tq, t_len), 0) + r0
            j = lax.broadcasted_iota(jnp.int32, (tq, t_len), 1)
            diff = (i - j).astype(F32)
            arg = jnp.where(diff > 0, lgf * diff, -lgb * diff)
            dmat = jnp.exp(arg) * jnp.where(diff == 0, 2.0, 1.0)
            qblk = qh[r0:r0 + tq]
            s = _dot_nt(qblk, kb) * dmat
            o = _dot(s.astype(BF16), vh)
            if latent:
                pos = (lax.broadcasted_iota(jnp.int32, (tq, 1), 0) + r0).astype(F32)
                o = o + _dot(qblk, s0_ref[0].astype(BF16)) * jnp.exp(lgf * (pos + 1.0))
                o = o + _dot(qblk, s0_ref[1].astype(BF16)) * jnp.exp(lgb * (t_len - pos))
            y = _ln_rows(o) * gng_ref[:, h * RET_DV:(h + 1) * RET_DV] + gnb_ref[:, h * RET_DV:(h + 1) * RET_DV]
            gate = g_ref[r0:r0 + tq, h * RET_DV:(h + 1) * RET_DV]
            y_ref[r0:r0 + tq, h * RET_DV:(h + 1) * RET_DV] = (y * _silu(gate)).astype(BF16)
        if not latent:
            pos = lax.broadcasted_iota(jnp.int32, (t_len, 1), 0).astype(F32)
            kf = (k * jnp.exp(lgf * (t_len - 1.0 - pos))).astype(BF16)
            kbw = (k * jnp.exp(lgb * pos)).astype(BF16)
            st_ref[0, h] = _dot_tn(kf, vh)[h * RET_DK:(h + 1) * RET_DK, :]
            st_ref[1, h] = _dot_tn(kbw, vh)[h * RET_DK:(h + 1) * RET_DK, :]


def _ret(proj, lg, gn_g, gn_b, *, latent, cos=None, sin=None, state=None, layer=None):
    t_len = DEC_SEQ if latent else SEQ
    nb = DEC_BATCH if latent else BATCH
    rb = N_PROMPT // DEC_SEQ if latent else 0
    pw = 2 * RET_DV
    in_specs = [
        pl.BlockSpec(memory_space=pltpu.SMEM),
        pl.BlockSpec((t_len, LANES), lambda b, p: (rb + b, COL_B_Q // LANES + p)),
        pl.BlockSpec((t_len, LANES), lambda b, p: (rb + b, COL_B_K // LANES + p)),
        pl.BlockSpec((t_len, pw), lambda b, p: (rb + b, COL_B_V // pw + p)),
        pl.BlockSpec((t_len, pw), lambda b, p: (rb + b, COL_B_G // pw + p)),
        pl.BlockSpec((1, pw), lambda b, p: (0, p)),
        pl.BlockSpec((1, pw), lambda b, p: (0, p)),
    ]
    args = [lg, proj, proj, proj, proj, gn_g, gn_b]
    out_specs = [pl.BlockSpec((t_len, pw), lambda b, p: (b, p))]
    out_shape = [jax.ShapeDtypeStruct((nb * t_len, RET_HEADS * RET_DV), BF16)]
    if latent:
        in_specs += [pl.BlockSpec((t_len, LANES), lambda b, p: (0, 0)),
                     pl.BlockSpec((t_len, LANES), lambda b, p: (0, 0)),
                     pl.BlockSpec((None, None, 2, None, 2 * RET_DK, RET_DV), lambda b, p: (b, layer, 0, p, 0, 0))]
        args += [cos, sin, state]
    else:
        out_specs.append(pl.BlockSpec((None, 2, 2, RET_DK, RET_DV), lambda b, p: (b, 0, p, 0, 0)))
        out_shape.append(jax.ShapeDtypeStruct((nb, 2, RET_HEADS, RET_DK, RET_DV), F32))
    return pl.pallas_call(
        functools.partial(_ret_kernel, t_len=t_len, latent=latent),
        grid=(nb, RET_HEADS // 2),
        in_specs=in_specs, out_specs=out_specs, out_shape=out_shape,
        compiler_params=_cparams(("arbitrary", "arbitrary"), 48),
        name="ret_latent" if latent else "ret_ctx")(*args)


def _gla_kernel(q_ref, k_ref, v_ref, g_ref, lr_ref, up_ref, ub_ref, ng_ref, *rest, t_len, latent):
    if latent:
        s0_ref, y_ref, la_scr, o_scr, st_scr = rest
    else:
        y_ref, stout_ref, la_scr, o_scr, st_scr = rest
    n_chunks = t_len // CHUNK
    lr = lr_ref[...].astype(BF16)
    for d in range(2):
        z = _dot(lr, up_ref[d].astype(BF16)) + ub_ref[d]
        la_scr[d] = (jnp.minimum(z, 0.0) - jnp.log(1.0 + jnp.exp(-jnp.abs(z)))) / GLA_NORMALIZER

    ci = lax.broadcasted_iota(jnp.int32, (CHUNK, CHUNK), 0)
    cj = lax.broadcasted_iota(jnp.int32, (CHUNK, CHUNK), 1)
    lane = lax.broadcasted_iota(jnp.int32, (CHUNK, LANES), 1)
    head_lanes = [(lane >= h * GLA_DK) & (lane < (h + 1) * GLA_DK) for h in range(2)]

    for d in range(2):
        keep = (cj <= ci) if d == 0 else (cj >= ci)
        tri = jnp.where(keep, 1.0, 0.0).astype(BF16)
        if latent:
            s0t = s0_ref[d].T
            st_scr[0] = s0t
            st_scr[1] = s0t
        else:
            st_scr[0] = jnp.zeros((GLA_DV, LANES), F32)
            st_scr[1] = jnp.zeros((GLA_DV, LANES), F32)

        def body(step, carry, d=d, keep=keep, tri=tri):
            c = step if d == 0 else n_chunks - 1 - step
            rows = pl.ds(pl.multiple_of(c * CHUNK, CHUNK), CHUNK)
            la = la_scr[d, rows, :]
            la_hi = la.astype(BF16)
            la_lo = (la - la_hi.astype(F32)).astype(BF16)
            bcum = _dot(tri, la_hi) + _dot(tri, la_lo)
            btot = bcum[CHUNK - 1:CHUNK, :] if d == 0 else bcum[0:1, :]
            q_in = q_ref[rows, :] * (GLA_DK ** -0.5) * jnp.exp(bcum)
            kk = k_ref[rows, :]
            k_in = (kk * jnp.exp(-bcum)).astype(BF16)
            k_out = (kk * jnp.exp(btot - bcum)).astype(BF16)
            dec = jnp.exp(btot)
            for h in range(2):
                qm = jnp.where(head_lanes[h], q_in, 0.0).astype(BF16)
                vh = v_ref[rows, h * GLA_DV:(h + 1) * GLA_DV].astype(BF16)
                sc = jnp.where(keep, _dot_nt(qm, k_in), 0.0).astype(BF16)
                st = st_scr[h]
                o = _dot(sc, vh) + _dot_nt(qm, st.astype(BF16))
                st_scr[h] = st * dec + _dot_tn(vh, k_out)
                if d == 0:
                    o_scr[rows, h * GLA_DV:(h + 1) * GLA_DV] = o
                else:
                    o_scr[rows, h * GLA_DV:(h + 1) * GLA_DV] += o
            return carry

        lax.fori_loop(0, n_chunks, body, 0, unroll=4)
        if not latent:
            for h in range(2):
                stout_ref[d, h] = st_scr[h].T[h * GLA_DK:(h + 1) * GLA_DK, :]

    for h in range(2):
        cols = slice(h * GLA_DV, (h + 1) * GLA_DV)
        y = _rms_rows(o_scr[:, cols]) * ng_ref[...]
        y_ref[:, cols] = (y.astype(F32) * _silu(g_ref[:, cols])).astype(BF16)


def _gla(proj, up_mat, up_bias, norm_g, *, latent, state=None, layer=None):
    t_len = DEC_SEQ if latent else SEQ
    nb = DEC_BATCH if latent else BATCH
    rb = N_PROMPT // DEC_SEQ if latent else 0
    pw = 2 * GLA_DV
    in_specs = [
        pl.BlockSpec((t_len, LANES), lambda b, p: (rb + b, COL_A_Q // LANES + p)),
        pl.BlockSpec((t_len, LANES), lambda b, p: (rb + b, COL_A_K // LANES + p)),
        pl.BlockSpec((t_len, pw), lambda b, p: (rb + b, COL_A_V // pw + p)),
        pl.BlockSpec((t_len, pw), lambda b, p: (rb + b, COL_A_G // pw + p)),
        pl.BlockSpec((t_len, LANES), lambda b, p: (rb + b, COL_A_LR // LANES)),
        pl.BlockSpec((2, LANES, LANES), lambda b, p: (0, 0, p)),
        pl.BlockSpec((2, 1, LANES), lambda b, p: (0, 0, p)),
        pl.BlockSpec((1, GLA_DV), lambda b, p: (0, 0)),
    ]
    args = [proj, proj, proj, proj, proj, up_mat, up_bias, norm_g]
    out_specs = [pl.BlockSpec((t_len, pw), lambda b, p: (b, p))]
    out_shape = [jax.ShapeDtypeStruct((nb * t_len, GLA_HEADS * GLA_DV), BF16)]
    if latent:
        in_specs.append(pl.BlockSpec((None, None, 2, None, 2 * GLA_DK, GLA_DV),
                                     lambda b, p: (b, layer, 0, p, 0, 0)))
        args.append(state)
    else:
        out_specs.append(pl.BlockSpec((None, 2, 2, GLA_DK, GLA_DV), lambda b, p: (b, 0, p, 0, 0)))
        out_shape.append(jax.ShapeDtypeStruct((nb, 2, GLA_HEADS, GLA_DK, GLA_DV), F32))
    scratch = [pltpu.VMEM((2, t_len, LANES), F32), pltpu.VMEM((t_len, pw), F32),
               pltpu.VMEM((2, GLA_DV, LANES), F32)]
    return pl.pallas_call(
        functools.partial(_gla_kernel, t_len=t_len, latent=latent),
        grid=(nb, GLA_HEADS // 2),
        in_specs=in_specs, out_specs=out_specs, out_shape=out_shape, scratch_shapes=scratch,
        compiler_params=_cparams(("arbitrary", "arbitrary"), 32),
        name="gla_latent" if latent else "gla_ctx")(*args)


def _router_kernel(h_ref, w_ref, b_ref, idx_ref, gate_ref, rank_ref, cnt_ref, run_scr):
    step = pl.program_id(0)

    @pl.when(step == 0)
    def _():
        run_scr[...] = jnp.zeros(run_scr.shape, F32)

    logits = _dot(h_ref[...], w_ref[...].astype(BF16)) + b_ref[...]
    lane = lax.broadcasted_iota(jnp.int32, logits.shape, 1)
    idx_out = jnp.zeros(logits.shape, jnp.int32)
    val_out = jnp.zeros(logits.shape, F32)
    member = jnp.zeros(logits.shape, F32)
    sels = []
    m0 = None
    denom = None
    for kk in range(TOP_K):
        m = logits.max(axis=-1, keepdims=True)
        sel = jnp.min(jnp.where(logits == m, lane, LANES), axis=-1, keepdims=True)
        sels.append(sel)
        if kk == 0:
            m0 = m
        e = jnp.exp(m - m0)
        denom = e if denom is None else denom + e
        idx_out = jnp.where(lane == kk, sel, idx_out)
        val_out = jnp.where(lane == kk, e, val_out)
        member = jnp.where(lane == sel, 1.0, member)
        logits = jnp.where(lane == sel, -jnp.inf, logits)
    idx_ref[...] = idx_out
    gate_ref[...] = val_out / denom

    ti = lax.broadcasted_iota(jnp.int32, (ROUTER_TM, ROUTER_TM), 0)
    tj = lax.broadcasted_iota(jnp.int32, (ROUTER_TM, ROUTER_TM), 1)
    earlier = jnp.where(tj < ti, 1.0, 0.0).astype(BF16)
    before = _dot(earlier, member.astype(BF16)) + run_scr[...]
    rank_out = jnp.zeros(logits.shape, F32)
    for kk in range(TOP_K):
        r = jnp.sum(jnp.where(lane == sels[kk], before, 0.0), axis=-1, keepdims=True)
        rank_out = jnp.where(lane == kk, r, rank_out)
    rank_ref[...] = rank_out.astype(jnp.int32)
    run_scr[...] = run_scr[...] + jnp.sum(member, axis=0, keepdims=True)
    cnt_ref[...] = run_scr[...]


def _router(h2, w_pad, b_pad):
    tm = ROUTER_TM
    tok_spec = pl.BlockSpec((tm, LANES), lambda i: (i, 0))
    return pl.pallas_call(
        _router_kernel,
        grid=(N_TOK // tm,),
        in_specs=[pl.BlockSpec((tm, D_MODEL), lambda i: (i, 0)),
                  pl.BlockSpec((D_MODEL, LANES), lambda i: (0, 0)),
                  pl.BlockSpec((1, LANES), lambda i: (0, 0))],
        out_specs=[tok_spec, tok_spec, tok_spec, pl.BlockSpec((1, LANES), lambda i: (0, 0))],
        out_shape=[jax.ShapeDtypeStruct((N_TOK, LANES), jnp.int32), jax.ShapeDtypeStruct((N_TOK, LANES), F32),
                   jax.ShapeDtypeStruct((N_TOK, LANES), jnp.int32), jax.ShapeDtypeStruct((1, LANES), F32)],
        scratch_shapes=[pltpu.VMEM((1, LANES), F32)],
        compiler_params=_cparams(("arbitrary",), 32),
        name="router")(h2, w_pad, b_pad)


def _tile_idx(i, nu_ref):
    return jnp.minimum(i, nu_ref[0] - 1)


def _run_start(te_ref, i):
    prev = te_ref[jnp.maximum(i - 1, 0)]
    return (i == 0) | (te_ref[i] != prev)


def _stream_weights(te_ref, nx_ref, copy_for, n_copies, on_ready):
    j = pl.program_id(0)
    i = pl.program_id(1)
    e = te_ref[i]

    @pl.when((i == 0) & (j == 0))
    def _():
        for c in range(n_copies):
            copy_for(e, j, c).start()

    for c in range(n_copies):
        copy_for(e, j, c).wait()
    on_ready()
    ne = nx_ref[e]
    jn = j + (ne <= e).astype(jnp.int32)

    @pl.when(jn < pl.num_programs(0))
    def _():
        for c in range(n_copies):
            copy_for(ne, jn, c).start()


def _moe_gu_kernel(te_ref, nu_ref, nx_ref, x_ref, w_hbm, bg_ref, bu_ref, h_ref, stage, wg_s, wu_s, sems, *, layer):
    i = pl.program_id(1)

    def copy_for(e, jj, which):
        col = pl.multiple_of(which * EXPERT_FF + jj * MOE_TF, MOE_TF)
        return pltpu.make_async_copy(w_hbm.at[layer, e, :, pl.ds(col, MOE_TF)], stage.at[which], sems.at[which])

    def cast():
        wg_s[...] = stage[0].astype(BF16)
        wu_s[...] = stage[1].astype(BF16)

    @pl.when(i < nu_ref[0])
    def _():
        @pl.when(_run_start(te_ref, i))
        def _():
            _stream_weights(te_ref, nx_ref, copy_for, 2, cast)

        x = x_ref[...]
        gate = _dot(x, wg_s[...]) + bg_ref[...]
        up = _dot(x, wu_s[...]) + bu_ref[...]
        gate = jnp.minimum(gate, SWIGLU_LIMIT)
        up = jnp.clip(up, -SWIGLU_LIMIT, SWIGLU_LIMIT)
        glu = gate * _sigmoid(SWIGLU_ALPHA * gate)
        h_ref[...] = ((up + 1.0) * glu).astype(BF16)

    @pl.when(i >= nu_ref[0])
    def _():
        h_ref[...] = jnp.zeros(h_ref.shape, BF16)


def _moe_gu(xs, w_gu, b_gu4, layer, tile_expert, n_used, next_expert):
    nj = EXPERT_FF // MOE_TF
    grid_spec = pltpu.PrefetchScalarGridSpec(
        num_scalar_prefetch=3,
        grid=(nj, MOE_TILES),
        in_specs=[
            pl.BlockSpec((MOE_TM, D_MODEL), lambda j, i, te, nu, nx: (_tile_idx(i, nu), 0)),
            pl.BlockSpec(memory_space=pl.ANY),
            pl.BlockSpec((None, None, 1, MOE_TF), lambda j, i, te, nu, nx: (layer, te[_tile_idx(i, nu)], 0, j)),
            pl.BlockSpec((None, None, 1, MOE_TF), lambda j, i, te, nu, nx: (layer, te[_tile_idx(i, nu)], 0, nj + j)),
        ],
        out_specs=pl.BlockSpec((MOE_TM, MOE_TF), lambda j, i, te, nu, nx: (i, j)),
        scratch_shapes=[pltpu.VMEM((2, D_MODEL, MOE_TF), F32), pltpu.VMEM((D_MODEL, MOE_TF), BF16),
                        pltpu.VMEM((D_MODEL, MOE_TF), BF16), pltpu.SemaphoreType.DMA((2,))])
    return pl.pallas_call(
        functools.partial(_moe_gu_kernel, layer=layer), grid_spec=grid_spec,
        out_shape=jax.ShapeDtypeStruct((MOE_SLOTS, EXPERT_FF), BF16),
        compiler_params=_cparams(("arbitrary", "arbitrary"), 44),
        name="moe_gate_up")(tile_expert, n_used, next_expert, xs, w_gu, b_gu4, b_gu4)


def _moe_down_kernel(te_ref, nu_ref, nx_ref, h_ref, w_hbm, b_ref, y_ref, stage, w_s, sems, *, layer):
    i = pl.program_id(1)

    def copy_for(e, jj, which):
        col = pl.multiple_of(jj * MOE_TN, MOE_TN)
        return pltpu.make_async_copy(w_hbm.at[layer, e, :, pl.ds(col, MOE_TN)], stage, sems.at[0])

    def cast():
        w_s[...] = stage[...].astype(BF16)

    @pl.when(i < nu_ref[0])
    def _():
        @pl.when(_run_start(te_ref, i))
        def _():
            _stream_weights(te_ref, nx_ref, copy_for, 1, cast)

        y = _dot(h_ref[...], w_s[...]) + b_ref[...]
        for c in range(Y_TILES):
            y_ref[pl.ds(c, MOE_TM, stride=Y_TILES), :] = y[:, c * LANES:(c + 1) * LANES]

    @pl.when(i >= nu_ref[0])
    def _():
        y_ref[...] = jnp.zeros(y_ref.shape, F32)


def _moe_down(hs, w_down, b_down4, layer, tile_expert, n_used, next_expert):
    grid_spec = pltpu.PrefetchScalarGridSpec(
        num_scalar_prefetch=3,
        grid=(Y_PARTS, MOE_TILES),
        in_specs=[
            pl.BlockSpec((MOE_TM, EXPERT_FF), lambda j, i, te, nu, nx: (_tile_idx(i, nu), 0)),
            pl.BlockSpec(memory_space=pl.ANY),
            pl.BlockSpec((None, None, 1, MOE_TN), lambda j, i, te, nu, nx: (layer, te[_tile_idx(i, nu)], 0, j)),
        ],
        out_specs=pl.BlockSpec((None, MOE_TM * Y_TILES, LANES), lambda j, i, te, nu, nx: (j, i, 0)),
        scratch_shapes=[pltpu.VMEM((EXPERT_FF, MOE_TN), F32), pltpu.VMEM((EXPERT_FF, MOE_TN), BF16),
                        pltpu.SemaphoreType.DMA((1,))])
    ys = pl.pallas_call(
        functools.partial(_moe_down_kernel, layer=layer), grid_spec=grid_spec,
        out_shape=jax.ShapeDtypeStruct((Y_PARTS, MOE_SLOTS * Y_TILES, LANES), F32),
        compiler_params=_cparams(("arbitrary", "arbitrary"), 44),
        name="moe_down")(tile_expert, n_used, next_expert, hs, w_down, b_down4)
    return ys.reshape(Y_PARTS, MOE_SLOTS, Y_TILES, LANES)


def _moe_plan(top_idx, rank, counts):
    experts = jnp.arange(N_EXPERTS, dtype=jnp.int32)
    padded = (counts + MOE_TM - 1) // MOE_TM * MOE_TM
    pad_end = jnp.cumsum(padded)
    pad_start = pad_end - padded
    onehot = top_idx[:, :, None] == experts[None, None, :]
    dest = jnp.sum(jnp.where(onehot, pad_start[None, None, :], 0), axis=-1).astype(jnp.int32) + rank
    tok = jnp.repeat(jnp.arange(N_TOK, dtype=jnp.int32), TOP_K)
    slot_tok = jnp.zeros((MOE_SLOTS,), jnp.int32).at[dest.reshape(-1)].set(tok, unique_indices=True)
    n_used = (pad_end[-1] // MOE_TM).astype(jnp.int32).reshape(1)
    tile_start = jnp.arange(MOE_TILES, dtype=jnp.int32) * MOE_TM
    tile_expert = jnp.minimum(jnp.sum(pad_end[None, :] <= tile_start[:, None], axis=1), N_EXPERTS - 1)
    has = counts > 0
    later = has[None, :] & (experts[None, :] > experts[:, None])
    nxt_later = jnp.min(jnp.where(later, experts[None, :], N_EXPERTS), axis=1)
    first = jnp.min(jnp.where(has, experts, N_EXPERTS))
    next_expert = jnp.where(nxt_later < N_EXPERTS, nxt_later, first).astype(jnp.int32)
    return slot_tok, dest, tile_expert.astype(jnp.int32), n_used, next_expert


def _moe(h2, h2_rows, lw, weights, layer):
    idx_pad, gate_pad, rank_pad, cnt = _router(h2, lw['router_w'], lw['router_b'])
    top_idx = idx_pad[:, :TOP_K]
    rank = rank_pad[:, :TOP_K]
    counts = cnt[0, :N_EXPERTS].astype(jnp.int32)
    slot_tok, dest, tile_expert, n_used, next_expert = _moe_plan(top_idx, rank, counts)
    xs = _gather_x(h2_rows, slot_tok)
    hs = _moe_gu(xs, weights['w_gu'], weights['b_gu'], layer, tile_expert, n_used, next_expert)
    ys = _moe_down(hs, weights['w_down'], weights['b_down'], layer, tile_expert, n_used, next_expert)
    return ys, dest, gate_pad


def _layer(x, cond16, lw, weights, layer, caches, tabs):
    state_gla, state_ret, cache_gqa_k, cache_gqa_v, cache_nat_k, cache_nat_v = caches
    mod = _adaln_mod(cond16, weights['mod_w'], weights['mod_b'], layer)
    mod3 = mod.reshape(16 * 6, 1, D_MODEL)

    h = _ln_mod(x, mod3)
    proj = _in_proj(h, lw['w_in'])

    ya_c, st_a = _gla(proj, lw['gla_up'], lw['gla_ub'], lw['gla_norm_g'], latent=False)
    ya_l = _gla(proj, lw['gla_up'], lw['gla_ub'], lw['gla_norm_g'], latent=True, state=state_gla, layer=layer)[0]
    yb_c, st_b = _ret(proj, tabs['ret_lg'], lw['ret_gn_g'], lw['ret_gn_b'], latent=False)
    yb_l = _ret(proj, tabs['ret_lg'], lw['ret_gn_g'], lw['ret_gn_b'], latent=True,
                cos=tabs['cos64'], sin=tabs['sin64'], state=state_ret, layer=layer)[0]
    yc_c, kc = _ctx_attn(proj, COL_C_Q, COL_C_K, COL_C_V, GQA_KV_HEADS, GQA_HEADS // GQA_KV_HEADS,
                         lw['gqa_qn_g'], lw['gqa_kn_g'], norm=True, name="gqa_ctx")
    yc_l = _gqa_lat(proj, cache_gqa_k, cache_gqa_v, layer, lw['gqa_qn_g'], lw['gqa_kn_g'],
                    tabs['cos128'], tabs['sin128'])
    yd_c = _ctx_attn(proj, COL_D_Q, COL_D_K, COL_D_V, NAT_HEADS, 1, lw['gqa_qn_g'], lw['gqa_kn_g'],
                     norm=False, name="nat_ctx")[0]
    yd_l = _nat_lat(proj, cache_nat_k, cache_nat_v, layer, lw['nat_bias'])

    mix = _out_proj([ya_c, yb_c, yc_c, yd_c], [ya_l, yb_l, yc_l, yd_l], weights['w_o'], layer)
    x1, h2, h2_rows = _res_ln1(x, mix, mod3, lw['ln1_g'], lw['ln1_b'])
    y_rows, dest, gate_pad = _moe(h2, h2_rows, lw, weights, layer)
    x2 = _res_ln2(x1, y_rows, dest, gate_pad, mod3, lw['ln2_g'], lw['ln2_b'])

    vc = proj[:N_PROMPT, COL_C_V:COL_C_V + GQA_KV_HEADS * GQA_DIM]
    kd = proj[:N_PROMPT, COL_D_K:COL_D_K + NAT_HEADS * NAT_DIM]
    vd = proj[:N_PROMPT, COL_D_V:COL_D_V + NAT_HEADS * NAT_DIM]
    new = (st_a, st_b,
           kc.reshape(BATCH, SEQ, GQA_KV_HEADS, GQA_DIM), vc.reshape(BATCH, SEQ, GQA_KV_HEADS, GQA_DIM),
           kd.reshape(BATCH, SEQ, NAT_HEADS, NAT_DIM), vd.reshape(BATCH, SEQ, NAT_HEADS, NAT_DIM))
    return x2, new


def _prep_layer_weights(l, w_in, gla_lr_up, gla_lr_b, gla_norm_g, ret_gn_g, ret_gn_b, gqa_qn_g, gqa_kn_g, nat_rpb,
                        ln1_g, ln1_b, router_w, router_b, ln2_g, ln2_b):
    wi = w_in[l]
    w_in_p = jnp.concatenate([wi[:, :ALR_SRC], wi[:, ALR_SRC + ALR_W:], wi[:, ALR_SRC:ALR_SRC + ALR_W],
                              jnp.zeros((D_MODEL, PROJ_PAD - wi.shape[1]), wi.dtype)], axis=1).astype(BF16)
    up = jnp.zeros((2, LANES, GLA_HEADS * GLA_DK), F32)
    for d in range(2):
        up = up.at[d, d * GLA_RANK:(d + 1) * GLA_RANK, :].set(gla_lr_up[l, d])
    rw = jnp.zeros((D_MODEL, LANES), F32).at[:, :N_EXPERTS].set(router_w[l])
    rb = jnp.full((1, LANES), NEG_BIG, F32).at[0, :N_EXPERTS].set(router_b[l])
    return {
        'w_in': w_in_p,
        'gla_up': up, 'gla_ub': gla_lr_b[l].reshape(2, 1, -1), 'gla_norm_g': gla_norm_g[l].reshape(1, -1),
        'ret_gn_g': ret_gn_g[l].reshape(1, -1), 'ret_gn_b': ret_gn_b[l].reshape(1, -1),
        'gqa_qn_g': gqa_qn_g[l].reshape(1, -1), 'gqa_kn_g': gqa_kn_g[l].reshape(1, -1),
        'nat_bias': _nat_bias_table(nat_rpb[l]),
        'ln1_g': ln1_g[l].reshape(1, -1), 'ln1_b': ln1_b[l].reshape(1, -1),
        'router_w': rw, 'router_b': rb,
        'ln2_g': ln2_g[l].reshape(1, -1), 'ln2_b': ln2_b[l].reshape(1, -1),
    }


def kernel(x_prompt, x_sample, state_gla, state_ret, cache_gqa_k, cache_gqa_v, cache_nat_k, cache_nat_v, c, c_ctx, mod_w, mod_b, w_in, gla_lr_up, gla_lr_b, gla_norm_g, ret_gn_g, ret_gn_b, gqa_qn_g, gqa_kn_g, nat_rpb, w_o, ln1_g, ln1_b, router_w, router_b, w_gu, b_gu, w_down, b_down, ln2_g, ln2_b):
    x = jnp.concatenate([x_prompt.reshape(N_PROMPT, D_MODEL), x_sample.reshape(N_SAMPLE, D_MODEL)], axis=0)
    cond16 = jnp.zeros((16, D_MODEL), F32).at[0].set(c_ctx).at[1:1 + DEC_BATCH].set(c)
    cos128, sin128 = _rope_tables(DEC_SEQ, GQA_DIM // 4)
    cos64, sin64 = _rope_tables(DEC_SEQ, RET_DK // 4)
    tabs = {'cos128': cos128, 'sin128': sin128, 'cos64': cos64, 'sin64': sin64,
            'ret_lg': jnp.asarray(_ret_log_decays(), F32)}
    weights = {'mod_w': mod_w, 'mod_b': mod_b.reshape(DEPTH, 1, -1), 'w_o': w_o,
               'w_gu': w_gu, 'b_gu': b_gu.reshape(DEPTH, N_EXPERTS, 1, -1),
               'w_down': w_down, 'b_down': b_down.reshape(DEPTH, N_EXPERTS, 1, -1)}
    caches = (state_gla.reshape(DEC_BATCH, DEPTH, 2, GLA_HEADS // 2, 2 * GLA_DK, GLA_DV),
              state_ret.reshape(DEC_BATCH, DEPTH, 2, RET_HEADS // 2, 2 * RET_DK, RET_DV),
              cache_gqa_k.reshape(DEC_BATCH, DEPTH, PAST_LEN, GQA_KV_HEADS * GQA_DIM),
              cache_gqa_v.reshape(DEC_BATCH, DEPTH, PAST_LEN, GQA_KV_HEADS * GQA_DIM),
              cache_nat_k.reshape(DEC_BATCH, DEPTH, PAST_LEN, NAT_HEADS * NAT_DIM),
              cache_nat_v.reshape(DEC_BATCH, DEPTH, PAST_LEN, NAT_HEADS * NAT_DIM))
    news = []
    for l in range(DEPTH):
        lw = _prep_layer_weights(l, w_in, gla_lr_up, gla_lr_b, gla_norm_g, ret_gn_g, ret_gn_b, gqa_qn_g, gqa_kn_g,
                                 nat_rpb, ln1_g, ln1_b, router_w, router_b, ln2_g, ln2_b)
        x, new = _layer(x, cond16, lw, weights, l, caches, tabs)
        news.append(new)
    y_p = x[:N_PROMPT].reshape(BATCH, SEQ, D_MODEL)
    y_s = x[N_PROMPT:].reshape(DEC_BATCH, DEC_SEQ, D_MODEL)
    stacked = tuple(jnp.stack([news[l][k] for l in range(DEPTH)], axis=1) for k in range(6))
    return (y_p, y_s) + stacked
```

```python
import functools

import numpy as np
import jax
import jax.numpy as jnp
from jax import lax
from jax.experimental import pallas as pl
from jax.experimental.pallas import tpu as pltpu

F32 = jnp.float32
BF16 = jnp.bfloat16
U32 = jnp.uint32
I32 = jnp.int32

D_MODEL = 4096
BATCH = 16
SEQ = 256
DEPTH = 2
DEC_BATCH = 8
DEC_SEQ = 1024
PAST_LEN = 512
GRID_W = 64
GLA_HEADS = 8
GLA_DK = 64
GLA_DV = 128
GLA_RANK = 16
GLA_NORMALIZER = 16.0
RET_HEADS = 8
RET_DK = 64
RET_DV = 128
GQA_HEADS = 8
GQA_KV_HEADS = 2
GQA_DIM = 128
NAT_HEADS = 8
NAT_DIM = 128
NAT_ROWS = 8
NAT_COLS = 16
CHUNK = 64
ROPE_THETA = 10000.0
LN_EPS = 1e-6
N_EXPERTS = 32
TOP_K = 4
EXPERT_FF = 2048
SWIGLU_LIMIT = 7.0
SWIGLU_ALPHA = 1.702
DEEPNORM_ALPHA = (2.0 * DEPTH) ** 0.25

N_PROMPT = BATCH * SEQ
N_SAMPLE = DEC_BATCH * DEC_SEQ
N_TOK = N_PROMPT + N_SAMPLE

LANES = 128
SUBLANES = 8
ROW_TILE = 256
NEG_BIG = -1e30

COL_A_Q, COL_A_K, COL_A_V, COL_A_G = 0, 512, 1024, 2048
COL_B_Q, COL_B_K, COL_B_V, COL_B_G = 3072, 3584, 4096, 5120
COL_C_Q, COL_C_K, COL_C_V = 6144, 7168, 7424
COL_D_Q, COL_D_K, COL_D_V = 7680, 8704, 9728
COL_A_LR = 10752
PROJ_PAD = 11264
ALR_SRC = 3072
ALR_W = 2 * GLA_RANK

MOE_TM = 256
NK = N_TOK * TOP_K
MOE_TILES = NK // MOE_TM + N_EXPERTS
MOE_SLOTS = MOE_TILES * MOE_TM
MOE_TF = 512
MOE_TN = 2048
ROUTER_TM = 512

HALF = D_MODEL // 2
X_TILES = HALF // LANES
Y_PARTS = D_MODEL // MOE_TN
Y_HALF = MOE_TN // 2
Y_TILES = Y_HALF // LANES
X_PITCH = X_TILES + SUBLANES
Y_PITCH = Y_TILES + 2 * SUBLANES


def _cparams(sem, vmem_mb):
    return pltpu.CompilerParams(dimension_semantics=sem, vmem_limit_bytes=int(vmem_mb * 1024 * 1024))


def _dot(a, b):
    return jnp.dot(a, b, preferred_element_type=F32)


def _dot_nt(a, b):
    return lax.dot_general(a, b, (((1,), (1,)), ((), ())), preferred_element_type=F32)


def _dot_tn(a, b):
    return lax.dot_general(a, b, (((0,), (0,)), ((), ())), preferred_element_type=F32)


def _sigmoid(x):
    return 1.0 / (1.0 + jnp.exp(-x))


def _silu(x):
    return x * _sigmoid(x)


def _ln_rows(x):
    mu = jnp.mean(x, axis=-1, keepdims=True)
    xc = x - mu
    var = jnp.mean(xc * xc, axis=-1, keepdims=True)
    return xc * lax.rsqrt(var + LN_EPS)


def _rms_rows(x):
    return x * lax.rsqrt(jnp.mean(x * x, axis=-1, keepdims=True) + LN_EPS)


def _pack_words(a, b):
    ab = pltpu.bitcast(a.astype(BF16).astype(F32), U32)
    bb = pltpu.bitcast(b.astype(BF16).astype(F32), U32)
    return (ab >> 16) | (bb & jnp.uint32(0xFFFF0000))


def _unpack_words(w):
    lo = pltpu.bitcast(w << 16, F32)
    hi = pltpu.bitcast(w & jnp.uint32(0xFFFF0000), F32)
    return lo, hi


def _mod_spec(which, tile):
    n_ctx_tiles = N_PROMPT // tile
    per_b = DEC_SEQ // tile

    def imap(i, *_):
        row = jnp.where(i < n_ctx_tiles, 0, 1 + (i - n_ctx_tiles) // per_b)
        return (row * 6 + which, 0, 0)
    return pl.BlockSpec((1, 1, D_MODEL), imap)


def _mm_kernel(a_ref, w_ref, b_ref, o_ref, *, silu_a):
    a = a_ref[...]
    if silu_a:
        a = _silu(a.astype(F32))
    acc = _dot(a.astype(BF16), w_ref[...].astype(BF16))
    o_ref[...] = acc + b_ref[...]


def _adaln_mod(cond16, mod_w, mod_b3, layer):
    k, n = mod_w.shape[1], mod_w.shape[2]
    tn = 512
    return pl.pallas_call(
        functools.partial(_mm_kernel, silu_a=True),
        grid=(1, n // tn),
        in_specs=[pl.BlockSpec((16, k), lambda i, j: (0, 0)),
                  pl.BlockSpec((None, k, tn), lambda i, j: (layer, 0, j)),
                  pl.BlockSpec((None, 1, tn), lambda i, j: (layer, 0, j))],
        out_specs=pl.BlockSpec((16, tn), lambda i, j: (0, j)),
        out_shape=jax.ShapeDtypeStruct((16, n), F32),
        compiler_params=_cparams(("arbitrary", "arbitrary"), 40),
        name="adaln_mod")(cond16, mod_w, mod_b3)


def _mm_plain_kernel(a_ref, w_ref, o_ref):
    o_ref[...] = _dot(a_ref[...], w_ref[...])


def _in_proj(h, w_in_p):
    tm, tn = 1024, 512
    k = h.shape[1]
    return pl.pallas_call(
        _mm_plain_kernel,
        grid=(N_TOK // tm, PROJ_PAD // tn),
        in_specs=[pl.BlockSpec((tm, k), lambda i, j: (i, 0)),
                  pl.BlockSpec((k, tn), lambda i, j: (0, j))],
        out_specs=pl.BlockSpec((tm, tn), lambda i, j: (i, j)),
        out_shape=jax.ShapeDtypeStruct((N_TOK, PROJ_PAD), F32),
        compiler_params=_cparams(("arbitrary", "arbitrary"), 40),
        name="in_proj")(h, w_in_p)


def _mm4_kernel(*refs, n_ctx_tiles):
    ctx, lat, ws, o_ref = refs[0:4], refs[4:8], refs[8:12], refs[12]

    def project(parts):
        acc = _dot(parts[0][...], ws[0][...].astype(BF16))
        for a, w in zip(parts[1:], ws[1:]):
            acc += _dot(a[...], w[...].astype(BF16))
        o_ref[...] = acc

    is_ctx = pl.program_id(0) < n_ctx_tiles
    pl.when(is_ctx)(lambda: project(ctx))
    pl.when(jnp.logical_not(is_ctx))(lambda: project(lat))


def _out_proj(ys_ctx, ys_lat, w_o, layer, *, tm=1024, tn=256):
    kp = ys_ctx[0].shape[1]
    n_ctx_tiles = N_PROMPT // tm
    ctx_specs = [pl.BlockSpec((tm, kp), lambda i, j: (jnp.minimum(i, n_ctx_tiles - 1), 0)) for _ in range(4)]
    lat_specs = [pl.BlockSpec((tm, kp), lambda i, j: (jnp.maximum(i - n_ctx_tiles, 0), 0)) for _ in range(4)]
    w_specs = [pl.BlockSpec((None, kp, tn), functools.partial(lambda i, j, p: (layer, p, j), p=p)) for p in range(4)]
    return pl.pallas_call(
        functools.partial(_mm4_kernel, n_ctx_tiles=n_ctx_tiles),
        grid=(N_TOK // tm, D_MODEL // tn),
        in_specs=ctx_specs + lat_specs + w_specs,
        out_specs=pl.BlockSpec((tm, tn), lambda i, j: (i, j)),
        out_shape=jax.ShapeDtypeStruct((N_TOK, D_MODEL), F32),
        compiler_params=_cparams(("arbitrary", "arbitrary"), 48),
        name="out_proj")(*ys_ctx, *ys_lat, w_o, w_o, w_o, w_o)


def _ln_mod_kernel(x_ref, sh_ref, sc_ref, h_ref):
    h = _ln_rows(x_ref[...]) * (1.0 + sc_ref[0]) + sh_ref[0]
    h_ref[...] = h.astype(BF16)


def _ln_mod(x, mod3):
    return pl.pallas_call(
        _ln_mod_kernel,
        grid=(N_TOK // ROW_TILE,),
        in_specs=[pl.BlockSpec((ROW_TILE, D_MODEL), lambda i: (i, 0)), _mod_spec(0, ROW_TILE),
                  _mod_spec(1, ROW_TILE)],
        out_specs=pl.BlockSpec((ROW_TILE, D_MODEL), lambda i: (i, 0)),
        out_shape=jax.ShapeDtypeStruct((N_TOK, D_MODEL), BF16),
        compiler_params=_cparams(("arbitrary",), 32),
        name="ln_mod")(x, mod3, mod3)


def _res_ln1_kernel(x_ref, mix_ref, gate_ref, g_ref, b_ref, sh_ref, sc_ref, x1_ref, h2_ref, rows_ref):
    y = DEEPNORM_ALPHA * x_ref[...] + gate_ref[0] * mix_ref[...]
    x1 = _ln_rows(y) * g_ref[...] + b_ref[...]
    x1_ref[...] = x1
    h2 = _ln_rows(x1) * (1.0 + sc_ref[0]) + sh_ref[0]
    h2_ref[...] = h2.astype(BF16)
    for c in range(X_TILES):
        lo = h2[:, c * LANES:(c + 1) * LANES]
        hi = h2[:, HALF + c * LANES:HALF + (c + 1) * LANES]
        rows_ref[pl.ds(c, ROW_TILE, stride=X_TILES), :] = _pack_words(lo, hi)


def _res_ln1(x, mix, mod3, ln_g, ln_b):
    tile = ROW_TILE
    row_spec = pl.BlockSpec((tile, D_MODEL), lambda i: (i, 0))
    vec_spec = pl.BlockSpec((1, D_MODEL), lambda i: (0, 0))
    x1, h2, rows = pl.pallas_call(
        _res_ln1_kernel,
        grid=(N_TOK // tile,),
        in_specs=[row_spec, row_spec, _mod_spec(2, tile), vec_spec, vec_spec, _mod_spec(3, tile), _mod_spec(4, tile)],
        out_specs=[row_spec, row_spec, pl.BlockSpec((tile * X_TILES, LANES), lambda i: (i, 0))],
        out_shape=[jax.ShapeDtypeStruct((N_TOK, D_MODEL), F32), jax.ShapeDtypeStruct((N_TOK, D_MODEL), BF16),
                   jax.ShapeDtypeStruct((N_TOK * X_TILES, LANES), U32)],
        compiler_params=_cparams(("arbitrary",), 48),
        name="res_ln_mixer")(x, mix, mod3, ln_g, ln_b, mod3, mod3)
    return x1, h2, rows.reshape(1, N_TOK, X_TILES, LANES)


def _gather_step(idx_cur, idx_nxt, src_ref, buf, sems, n_rows, tiles, pitch):
    step = pl.program_id(0)
    slot = step % 2

    def issue(idx_ref, to_slot):
        def body(pair, carry):
            for queue in range(2):
                r = pair * 2 + queue
                dst = buf.at[to_slot, :, pl.ds(pl.multiple_of(r * pitch, SUBLANES), tiles), :]
                pltpu.make_async_copy(src_ref.at[:, idx_ref[0, 0, r]], dst, sems.at[to_slot]).start(priority=queue)
            return carry
        lax.fori_loop(0, n_rows // 2, body, 0, unroll=4)

    @pl.when(step == 0)
    def _():
        issue(idx_cur, slot)

    @pl.when(step + 1 < pl.num_programs(0))
    def _():
        issue(idx_nxt, 1 - slot)

    landed = buf.at[slot, :, pl.ds(0, n_rows * tiles), :]
    pltpu.make_async_copy(landed, landed, sems.at[slot]).wait()
    return slot


def _idx_specs(n_rows, n_steps):
    return [pl.BlockSpec((1, 1, n_rows), lambda i: (i, 0, 0), memory_space=pltpu.SMEM),
            pl.BlockSpec((1, 1, n_rows), lambda i: (jnp.minimum(i + 1, n_steps - 1), 0, 0),
                         memory_space=pltpu.SMEM)]


def _res_ln2_kernel(idx_cur, idx_nxt, x_ref, y_hbm, gates_ref, gate_ref, g_ref, b_ref, o_ref, buf, sems, mix_scr,
                    *, tile):
    slot = _gather_step(idx_cur, idx_nxt, y_hbm, buf, sems, TOP_K * tile, Y_TILES, Y_PITCH)
    gates = gates_ref[...]
    for part in range(Y_PARTS):
        for c in range(Y_TILES):
            acc_lo = None
            acc_hi = None
            for kk in range(TOP_K):
                lo, hi = _unpack_words(buf[slot, part, pl.ds(kk * tile * Y_PITCH + c, tile, stride=Y_PITCH), :])
                g = gates[:, kk:kk + 1]
                acc_lo = g * lo if acc_lo is None else acc_lo + g * lo
                acc_hi = g * hi if acc_hi is None else acc_hi + g * hi
            col = part * MOE_TN + c * LANES
            mix_scr[:, col:col + LANES] = acc_lo
            mix_scr[:, col + Y_HALF:col + Y_HALF + LANES] = acc_hi
    y = DEEPNORM_ALPHA * x_ref[...] + gate_ref[0] * mix_scr[...]
    o_ref[...] = _ln_rows(y) * g_ref[...] + b_ref[...]


def _res_ln2(x1, y_rows, dest, gates_pad, mod3, ln_g, ln_b):
    tile = 64
    n_tiles = N_TOK // tile
    idx = dest.reshape(n_tiles, tile, TOP_K).transpose(0, 2, 1).reshape(n_tiles, 1, TOP_K * tile)
    row_spec = pl.BlockSpec((tile, D_MODEL), lambda i: (i, 0))
    vec_spec = pl.BlockSpec((1, D_MODEL), lambda i: (0, 0))
    return pl.pallas_call(
        functools.partial(_res_ln2_kernel, tile=tile),
        grid=(n_tiles,),
        in_specs=_idx_specs(TOP_K * tile, n_tiles) + [row_spec, pl.BlockSpec(memory_space=pl.ANY),
                                                      pl.BlockSpec((tile, LANES), lambda i: (i, 0)),
                                                      _mod_spec(5, tile), vec_spec, vec_spec],
        out_specs=row_spec,
        out_shape=jax.ShapeDtypeStruct((N_TOK, D_MODEL), F32),
        scratch_shapes=[pltpu.VMEM((2, Y_PARTS, TOP_K * tile * Y_PITCH, LANES), U32), pltpu.SemaphoreType.DMA((2,)),
                        pltpu.VMEM((tile, D_MODEL), F32)],
        compiler_params=_cparams(("arbitrary",), 40),
        name="res_ln_moe")(idx, idx, x1, y_rows, gates_pad, mod3, ln_g, ln_b)


def _gather_x_kernel(idx_cur, idx_nxt, src_hbm, o_ref, buf, sems):
    slot = _gather_step(idx_cur, idx_nxt, src_hbm, buf, sems, MOE_TM, X_TILES, X_PITCH)
    for c in range(X_TILES):
        lo, hi = _unpack_words(buf[slot, 0, pl.ds(c, MOE_TM, stride=X_PITCH), :])
        o_ref[:, c * LANES:(c + 1) * LANES] = lo.astype(BF16)
        o_ref[:, HALF + c * LANES:HALF + (c + 1) * LANES] = hi.astype(BF16)


def _gather_x(h2_rows, slot_tok):
    idx = slot_tok.reshape(MOE_TILES, 1, MOE_TM)
    return pl.pallas_call(
        _gather_x_kernel,
        grid=(MOE_TILES,),
        in_specs=_idx_specs(MOE_TM, MOE_TILES) + [pl.BlockSpec(memory_space=pl.ANY)],
        out_specs=pl.BlockSpec((MOE_TM, D_MODEL), lambda i: (i, 0)),
        out_shape=jax.ShapeDtypeStruct((MOE_SLOTS, D_MODEL), BF16),
        scratch_shapes=[pltpu.VMEM((2, 1, MOE_TM * X_PITCH, LANES), U32), pltpu.SemaphoreType.DMA((2,))],
        compiler_params=_cparams(("arbitrary",), 32),
        name="moe_gather_x")(idx, idx, h2_rows)


def _rope(x, cos, sin_signed, quarter):
    lane = lax.broadcasted_iota(jnp.int32, x.shape, 1)
    up = pltpu.roll(x, LANES - quarter, axis=1)
    dn = pltpu.roll(x, quarter, axis=1)
    partner = jnp.where((lane % (2 * quarter)) < quarter, up, dn)
    return x * cos + partner * sin_signed


def _rope_tables(t_len, quarter):
    lane = np.arange(LANES)
    within = lane % (4 * quarter)
    is_col = within >= 2 * quarter
    j = within % quarter
    second = (within % (2 * quarter)) >= quarter
    freqs = ROPE_THETA ** (-(j.astype(np.float32)) / quarter)
    t = np.arange(t_len)
    pos = np.where(is_col[None, :], (t % GRID_W)[:, None], (t // GRID_W)[:, None]).astype(np.float32)
    ang = pos * freqs[None, :].astype(np.float32)
    cos = np.cos(ang).astype(np.float32)
    sin = np.sin(ang).astype(np.float32) * np.where(second, 1.0, -1.0)[None, :].astype(np.float32)
    return jnp.asarray(cos), jnp.asarray(sin)


def _attend(q, keys, vals, scale, biases=None):
    ss = []
    for idx, k in enumerate(keys):
        s = _dot_nt(q, k) * scale
        if biases is not None and biases[idx] is not None:
            s = s + biases[idx]
        ss.append(s)
    m = ss[0].max(axis=-1, keepdims=True)
    for s in ss[1:]:
        m = jnp.maximum(m, s.max(axis=-1, keepdims=True))
    l = None
    o = None
    for s, v in zip(ss, vals):
        p = jnp.exp(s - m)
        pl_sum = p.sum(axis=-1, keepdims=True)
        po = _dot(p.astype(BF16), v)
        l = pl_sum if l is None else l + pl_sum
        o = po if o is None else o + po
    return o / l


def _ctx_attn_kernel(q_ref, k_ref, v_ref, gq_ref, gk_ref, o_ref, *rest, groups, norm):
    k = k_ref[...]
    if norm:
        k = _rms_rows(k) * gk_ref[...]
        rest[0][...] = k
    kb = k.astype(BF16)
    vb = v_ref[...].astype(BF16)
    scale = GQA_DIM ** -0.5
    for g in range(groups):
        q = q_ref[:, g * LANES:(g + 1) * LANES]
        if norm:
            q = _rms_rows(q) * gq_ref[...]
        o_ref[:, g * LANES:(g + 1) * LANES] = _attend(q.astype(BF16), [kb], [vb], scale).astype(BF16)


def _ctx_attn(proj, col_q, col_k, col_v, n_kv, groups, gq, gk, *, norm, name):
    qw = groups * LANES
    in_specs = [pl.BlockSpec((SEQ, qw), lambda b, h: (b, col_q // qw + h)),
                pl.BlockSpec((SEQ, LANES), lambda b, h: (b, col_k // LANES + h)),
                pl.BlockSpec((SEQ, LANES), lambda b, h: (b, col_v // LANES + h)),
                pl.BlockSpec((1, LANES), lambda b, h: (0, 0)),
                pl.BlockSpec((1, LANES), lambda b, h: (0, 0))]
    out_specs = [pl.BlockSpec((SEQ, qw), lambda b, h: (b, h))]
    out_shape = [jax.ShapeDtypeStruct((N_PROMPT, n_kv * qw), BF16)]
    if norm:
        out_specs.append(pl.BlockSpec((SEQ, LANES), lambda b, h: (b, h)))
        out_shape.append(jax.ShapeDtypeStruct((N_PROMPT, n_kv * LANES), F32))
    return pl.pallas_call(
        functools.partial(_ctx_attn_kernel, groups=groups, norm=norm),
        grid=(BATCH, n_kv),
        in_specs=in_specs, out_specs=out_specs, out_shape=out_shape,
        compiler_params=_cparams(("arbitrary", "arbitrary"), 32),
        name=name)(proj, proj, proj, gq, gk)


def _gqa_lat_kernel(q_ref, k_ref, v_ref, ck_ref, cv_ref, gq_ref, gk_ref, cosq_ref, sinq_ref, cosk_ref, sink_ref,
                    o_ref, k_scr, v_scr):
    qb = pl.program_id(2)

    @pl.when(qb == 0)
    def _():
        k = _rms_rows(k_ref[...]) * gk_ref[...]
        k_scr[...] = _rope(k, cosk_ref[...], sink_ref[...], GQA_DIM // 4).astype(BF16)
        v_scr[...] = v_ref[...].astype(BF16)

    ck = ck_ref[...].astype(BF16)
    cv = cv_ref[...].astype(BF16)
    scale = GQA_DIM ** -0.5
    for g in range(GQA_HEADS // GQA_KV_HEADS):
        q = _rms_rows(q_ref[:, g * LANES:(g + 1) * LANES]) * gq_ref[...]
        q = _rope(q, cosq_ref[...], sinq_ref[...], GQA_DIM // 4).astype(BF16)
        o = _attend(q, [k_scr[...], ck], [v_scr[...], cv], scale)
        o_ref[:, g * LANES:(g + 1) * LANES] = o.astype(BF16)


def _gqa_lat(proj, cache_k, cache_v, layer, gq, gk, cos, sin):
    groups = GQA_HEADS // GQA_KV_HEADS
    qw = groups * LANES
    tq = ROW_TILE
    nqb = DEC_SEQ // tq
    row0 = N_PROMPT // tq
    in_specs = [
        pl.BlockSpec((tq, qw), lambda b, h, i: (row0 + b * nqb + i, COL_C_Q // qw + h)),
        pl.BlockSpec((DEC_SEQ, LANES), lambda b, h, i: (N_PROMPT // DEC_SEQ + b, COL_C_K // LANES + h)),
        pl.BlockSpec((DEC_SEQ, LANES), lambda b, h, i: (N_PROMPT // DEC_SEQ + b, COL_C_V // LANES + h)),
        pl.BlockSpec((None, None, PAST_LEN, LANES), lambda b, h, i: (b, layer, 0, h)),
        pl.BlockSpec((None, None, PAST_LEN, LANES), lambda b, h, i: (b, layer, 0, h)),
        pl.BlockSpec((1, LANES), lambda b, h, i: (0, 0)),
        pl.BlockSpec((1, LANES), lambda b, h, i: (0, 0)),
        pl.BlockSpec((tq, LANES), lambda b, h, i: (i, 0)),
        pl.BlockSpec((tq, LANES), lambda b, h, i: (i, 0)),
        pl.BlockSpec((DEC_SEQ, LANES), lambda b, h, i: (0, 0)),
        pl.BlockSpec((DEC_SEQ, LANES), lambda b, h, i: (0, 0)),
    ]
    return pl.pallas_call(
        _gqa_lat_kernel,
        grid=(DEC_BATCH, GQA_KV_HEADS, nqb),
        in_specs=in_specs,
        out_specs=pl.BlockSpec((tq, qw), lambda b, h, i: (b * nqb + i, h)),
        out_shape=jax.ShapeDtypeStruct((N_SAMPLE, GQA_HEADS * GQA_DIM), BF16),
        scratch_shapes=[pltpu.VMEM((DEC_SEQ, LANES), BF16), pltpu.VMEM((DEC_SEQ, LANES), BF16)],
        compiler_params=_cparams(("arbitrary", "arbitrary", "arbitrary"), 40),
        name="gqa_latent")(proj, proj, proj, cache_k, cache_v, gq, gk, cos, sin, cos, sin)


NAT_GRID_ROWS = DEC_SEQ // GRID_W
NAT_KR = min(NAT_ROWS, NAT_GRID_ROWS)
NAT_BAND = NAT_KR * GRID_W
NAT_PATTERNS = NAT_KR


def _nat_lat_kernel(q_ref, k_ref, v_ref, ck_ref, cv_ref, bias_ref, o_ref, k_scr, v_scr, ck_scr, cv_scr):
    k_scr[...] = k_ref[...].astype(BF16)
    v_scr[...] = v_ref[...].astype(BF16)
    ck_scr[...] = ck_ref[...].astype(BF16)
    cv_scr[...] = cv_ref[...].astype(BF16)
    scale = NAT_DIM ** -0.5

    def body(r, carry):
        band = jnp.clip(r - NAT_KR // 2, 0, NAT_GRID_ROWS - NAT_KR)
        start = pl.multiple_of(band * GRID_W, GRID_W)
        rows = pl.ds(pl.multiple_of(r * GRID_W, GRID_W), GRID_W)
        q = q_ref[rows, :].astype(BF16)
        kb = k_scr[pl.ds(start, NAT_BAND), :]
        vb = v_scr[pl.ds(start, NAT_BAND), :]
        bias = bias_ref[r - band]
        o = _attend(q, [kb, ck_scr[...]], [vb, cv_scr[...]], scale, biases=[bias, None])
        o_ref[rows, :] = o.astype(BF16)
        return carry

    lax.fori_loop(0, NAT_GRID_ROWS, body, 0, unroll=8)


def _nat_lat(proj, cache_k, cache_v, layer, bias_tab):
    rb = N_PROMPT // DEC_SEQ
    in_specs = [
        pl.BlockSpec((DEC_SEQ, LANES), lambda b, h: (rb + b, COL_D_Q // LANES + h)),
        pl.BlockSpec((DEC_SEQ, LANES), lambda b, h: (rb + b, COL_D_K // LANES + h)),
        pl.BlockSpec((DEC_SEQ, LANES), lambda b, h: (rb + b, COL_D_V // LANES + h)),
        pl.BlockSpec((None, None, PAST_LEN, LANES), lambda b, h: (b, layer, 0, h)),
        pl.BlockSpec((None, None, PAST_LEN, LANES), lambda b, h: (b, layer, 0, h)),
        pl.BlockSpec((None, NAT_PATTERNS, GRID_W, NAT_BAND), lambda b, h: (h, 0, 0, 0)),
    ]
    return pl.pallas_call(
        _nat_lat_kernel,
        grid=(DEC_BATCH, NAT_HEADS),
        in_specs=in_specs,
        out_specs=pl.BlockSpec((DEC_SEQ, LANES), lambda b, h: (b, h)),
        out_shape=jax.ShapeDtypeStruct((N_SAMPLE, NAT_HEADS * NAT_DIM), BF16),
        scratch_shapes=[pltpu.VMEM((DEC_SEQ, LANES), BF16), pltpu.VMEM((DEC_SEQ, LANES), BF16),
                        pltpu.VMEM((PAST_LEN, LANES), BF16), pltpu.VMEM((PAST_LEN, LANES), BF16)],
        compiler_params=_cparams(("arbitrary", "arbitrary"), 32),
        name="nat_latent")(proj, proj, proj, cache_k, cache_v, bias_tab)


def _nat_bias_table(rpb):
    off = np.arange(NAT_PATTERNS)
    kr = np.arange(NAT_KR)
    rel_r = kr[None, :] - off[:, None] + NAT_ROWS - 1
    c = np.arange(GRID_W)
    rel_c = np.clip(c[None, :] - c[:, None], 1 - NAT_COLS, NAT_COLS - 1) + NAT_COLS - 1
    col_start = np.clip(c - NAT_COLS // 2, 0, GRID_W - NAT_COLS)
    col_ok = (c[None, :] >= col_start[:, None]) & (c[None, :] < col_start[:, None] + NAT_COLS)
    n_pk = NAT_PATTERNS * NAT_KR
    row_sel = np.zeros((n_pk, 2 * NAT_ROWS - 1), np.float32)
    row_sel[np.arange(n_pk), rel_r.reshape(-1)] = 1.0
    col_sel = np.zeros((2 * NAT_COLS - 1, GRID_W * GRID_W), np.float32)
    col_sel[rel_c.reshape(-1), np.arange(GRID_W * GRID_W)] = 1.0
    bias = jnp.einsum('ka,hab,bx->hkx', jnp.asarray(row_sel), rpb.astype(F32), jnp.asarray(col_sel),
                      precision=lax.Precision.HIGHEST)
    bias = bias.reshape(NAT_HEADS, NAT_PATTERNS, NAT_KR, GRID_W, GRID_W).transpose(0, 1, 3, 2, 4)
    bias = jnp.where(jnp.asarray(col_ok)[None, None, :, None, :], bias, NEG_BIG)
    return bias.reshape(NAT_HEADS, NAT_PATTERNS, GRID_W, NAT_BAND)


def _ret_log_decays():
    expo = -5.0 - np.arange(RET_HEADS, dtype=np.float64)
    return np.log1p(-np.exp2(expo))


def _ret_kernel(lg_ref, q_ref, k_ref, v_ref, g_ref, gng_ref, gnb_ref, *rest, t_len, latent):
    if latent:
        cos_ref, sin_ref, s0_ref, y_ref = rest
    else:
        y_ref, st_ref = rest
    p = pl.program_id(1)
    q = q_ref[...]
    k = k_ref[...] * (RET_DK ** -0.5)
    if latent:
        q = _rope(q, cos_ref[...], sin_ref[...], RET_DK // 4)
        k = _rope(k, cos_ref[...], sin_ref[...], RET_DK // 4)
    kb = k.astype(BF16)
    lane = lax.broadcasted_iota(jnp.int32, (t_len, LANES), 1)
    tq = ROW_TILE
    for h in range(2):
        lgf = lg_ref[2 * p + h]
        lgb = lg_ref[RET_HEADS - 1 - (2 * p + h)]
        head_lanes = (lane >= h * RET_DK) & (lane < (h + 1) * RET_DK)
        qh = jnp.where(head_lanes, q, 0.0).astype(BF16)
        vh = v_ref[:, h * RET_DV:(h + 1) * RET_DV].astype(BF16)
        for blk in range(t_len // tq):
            r0 = blk * tq
            i = lax.broadcasted_iota(jnp.int32, (tq, t_len), 0) + r0
            j = lax.broadcasted_iota(jnp.int32, (tq, t_len), 1)
            diff = (i - j).astype(F32)
            arg = jnp.where(diff > 0, lgf * diff, -lgb * diff)
            dmat = jnp.exp(arg) * jnp.where(diff == 0, 2.0, 1.0)
            qblk = qh[r0:r0 + tq]
            s = _dot_nt(qblk, kb) * dmat
            o = _dot(s.astype(BF16), vh)
            if latent:
                pos = (lax.broadcasted_iota(jnp.int32, (tq, 1), 0) + r0).astype(F32)
                o = o + _dot(qblk, s0_ref[0].astype(BF16)) * jnp.exp(lgf * (pos + 1.0))
                o = o + _dot(qblk, s0_ref[1].astype(BF16)) * jnp.exp(lgb * (t_len - pos))
            y = _ln_rows(o) * gng_ref[:, h * RET_DV:(h + 1) * RET_DV] + gnb_ref[:, h * RET_DV:(h + 1) * RET_DV]
            gate = g_ref[r0:r0 + tq, h * RET_DV:(h + 1) * RET_DV]
            y_ref[r0:r0 + tq, h * RET_DV:(h + 1) * RET_DV] = (y * _silu(gate)).astype(BF16)
        if not latent:
            pos = lax.broadcasted_iota(jnp.int32, (t_len, 1), 0).astype(F32)
            kf = (k * jnp.exp(lgf * (t_len - 1.0 - pos))).astype(BF16)
            kbw = (k * jnp.exp(lgb * pos)).astype(BF16)
            st_ref[0, h] = _dot_tn(kf, vh)[h * RET_DK:(h + 1) * RET_DK, :]
            st_ref[1, h] = _dot_tn(kbw, vh)[h * RET_DK:(h + 1) * RET_DK, :]


def _ret(proj, lg, gn_g, gn_b, *, latent, cos=None, sin=None, state=None, layer=None):
    t_len = DEC_SEQ if latent else SEQ
    nb = DEC_BATCH if latent else BATCH
    rb = N_PROMPT // DEC_SEQ if latent else 0
    pw = 2 * RET_DV
    in_specs = [
        pl.BlockSpec(memory_space=pltpu.SMEM),
        pl.BlockSpec((t_len, LANES), lambda b, p: (rb + b, COL_B_Q // LANES + p)),
        pl.BlockSpec((t_len, LANES), lambda b, p: (rb + b, COL_B_K // LANES + p)),
        pl.BlockSpec((t_len, pw), lambda b, p: (rb + b, COL_B_V // pw + p)),
        pl.BlockSpec((t_len, pw), lambda b, p: (rb + b, COL_B_G // pw + p)),
        pl.BlockSpec((1, pw), lambda b, p: (0, p)),
        pl.BlockSpec((1, pw), lambda b, p: (0, p)),
    ]
    args = [lg, proj, proj, proj, proj, gn_g, gn_b]
    out_specs = [pl.BlockSpec((t_len, pw), lambda b, p: (b, p))]
    out_shape = [jax.ShapeDtypeStruct((nb * t_len, RET_HEADS * RET_DV), BF16)]
    if latent:
        in_specs += [pl.BlockSpec((t_len, LANES), lambda b, p: (0, 0)),
                     pl.BlockSpec((t_len, LANES), lambda b, p: (0, 0)),
                     pl.BlockSpec((None, None, 2, None, 2 * RET_DK, RET_DV), lambda b, p: (b, layer, 0, p, 0, 0))]
        args += [cos, sin, state]
    else:
        out_specs.append(pl.BlockSpec((None, 2, 2, RET_DK, RET_DV), lambda b, p: (b, 0, p, 0, 0)))
        out_shape.append(jax.ShapeDtypeStruct((nb, 2, RET_HEADS, RET_DK, RET_DV), F32))
    return pl.pallas_call(
        functools.partial(_ret_kernel, t_len=t_len, latent=latent),
        grid=(nb, RET_HEADS // 2),
        in_specs=in_specs, out_specs=out_specs, out_shape=out_shape,
        compiler_params=_cparams(("arbitrary", "arbitrary"), 48),
        name="ret_latent" if latent else "ret_ctx")(*args)


def _gla_kernel(q_ref, k_ref, v_ref, g_ref, lr_ref, up_ref, ub_ref, ng_ref, *rest, t_len, latent):
    if latent:
        s0_ref, y_ref, la_scr, o_scr, st_scr = rest
    else:
        y_ref, stout_ref, la_scr, o_scr, st_scr = rest
    n_chunks = t_len // CHUNK
    lr = lr_ref[...].astype(BF16)
    for d in range(2):
        z = _dot(lr, up_ref[d].astype(BF16)) + ub_ref[d]
        la_scr[d] = (jnp.minimum(z, 0.0) - jnp.log(1.0 + jnp.exp(-jnp.abs(z)))) / GLA_NORMALIZER

    ci = lax.broadcasted_iota(jnp.int32, (CHUNK, CHUNK), 0)
    cj = lax.broadcasted_iota(jnp.int32, (CHUNK, CHUNK), 1)
    lane = lax.broadcasted_iota(jnp.int32, (CHUNK, LANES), 1)
    head_lanes = [(lane >= h * GLA_DK) & (lane < (h + 1) * GLA_DK) for h in range(2)]

    for d in range(2):
        keep = (cj <= ci) if d == 0 else (cj >= ci)
        tri = jnp.where(keep, 1.0, 0.0).astype(BF16)
        if latent:
            s0t = s0_ref[d].T
            st_scr[0] = s0t
            st_scr[1] = s0t
        else:
            st_scr[0] = jnp.zeros((GLA_DV, LANES), F32)
            st_scr[1] = jnp.zeros((GLA_DV, LANES), F32)

        def body(step, carry, d=d, keep=keep, tri=tri):
            c = step if d == 0 else n_chunks - 1 - step
            rows = pl.ds(pl.multiple_of(c * CHUNK, CHUNK), CHUNK)
            la = la_scr[d, rows, :]
            la_hi = la.astype(BF16)
            la_lo = (la - la_hi.astype(F32)).astype(BF16)
            bcum = _dot(tri, la_hi) + _dot(tri, la_lo)
            btot = bcum[CHUNK - 1:CHUNK, :] if d == 0 else bcum[0:1, :]
            q_in = q_ref[rows, :] * (GLA_DK ** -0.5) * jnp.exp(bcum)
            kk = k_ref[rows, :]
            k_in = (kk * jnp.exp(-bcum)).astype(BF16)
            k_out = (kk * jnp.exp(btot - bcum)).astype(BF16)
            dec = jnp.exp(btot)
            for h in range(2):
                qm = jnp.where(head_lanes[h], q_in, 0.0).astype(BF16)
                vh = v_ref[rows, h * GLA_DV:(h + 1) * GLA_DV].astype(BF16)
                sc = jnp.where(keep, _dot_nt(qm, k_in), 0.0).astype(BF16)
                st = st_scr[h]
                o = _dot(sc, vh) + _dot_nt(qm, st.astype(BF16))
                st_scr[h] = st * dec + _dot_tn(vh, k_out)
                if d == 0:
                    o_scr[rows, h * GLA_DV:(h + 1) * GLA_DV] = o
                else:
                    o_scr[rows, h * GLA_DV:(h + 1) * GLA_DV] += o
            return carry

        lax.fori_loop(0, n_chunks, body, 0, unroll=4)
        if not latent:
            for h in range(2):
                stout_ref[d, h] = st_scr[h].T[h * GLA_DK:(h + 1) * GLA_DK, :]

    for h in range(2):
        cols = slice(h * GLA_DV, (h + 1) * GLA_DV)
        y = _rms_rows(o_scr[:, cols]) * ng_ref[...]
        y_ref[:, cols] = (y.astype(F32) * _silu(g_ref[:, cols])).astype(BF16)


def _gla(proj, up_mat, up_bias, norm_g, *, latent, state=None, layer=None):
    t_len = DEC_SEQ if latent else SEQ
    nb = DEC_BATCH if latent else BATCH
    rb = N_PROMPT // DEC_SEQ if latent else 0
    pw = 2 * GLA_DV
    in_specs = [
        pl.BlockSpec((t_len, LANES), lambda b, p: (rb + b, COL_A_Q // LANES + p)),
        pl.BlockSpec((t_len, LANES), lambda b, p: (rb + b, COL_A_K // LANES + p)),
        pl.BlockSpec((t_len, pw), lambda b, p: (rb + b, COL_A_V // pw + p)),
        pl.BlockSpec((t_len, pw), lambda b, p: (rb + b, COL_A_G // pw + p)),
        pl.BlockSpec((t_len, LANES), lambda b, p: (rb + b, COL_A_LR // LANES)),
        pl.BlockSpec((2, LANES, LANES), lambda b, p: (0, 0, p)),
        pl.BlockSpec((2, 1, LANES), lambda b, p: (0, 0, p)),
        pl.BlockSpec((1, GLA_DV), lambda b, p: (0, 0)),
    ]
    args = [proj, proj, proj, proj, proj, up_mat, up_bias, norm_g]
    out_specs = [pl.BlockSpec((t_len, pw), lambda b, p: (b, p))]
    out_shape = [jax.ShapeDtypeStruct((nb * t_len, GLA_HEADS * GLA_DV), BF16)]
    if latent:
        in_specs.append(pl.BlockSpec((None, None, 2, None, 2 * GLA_DK, GLA_DV),
                                     lambda b, p: (b, layer, 0, p, 0, 0)))
        args.append(state)
    else:
        out_specs.append(pl.BlockSpec((None, 2, 2, GLA_DK, GLA_DV), lambda b, p: (b, 0, p, 0, 0)))
        out_shape.append(jax.ShapeDtypeStruct((nb, 2, GLA_HEADS, GLA_DK, GLA_DV), F32))
    scratch = [pltpu.VMEM((2, t_len, LANES), F32), pltpu.VMEM((t_len, pw), F32),
               pltpu.VMEM((2, GLA_DV, LANES), F32)]
    return pl.pallas_call(
        functools.partial(_gla_kernel, t_len=t_len, latent=latent),
        grid=(nb, GLA_HEADS // 2),
        in_specs=in_specs, out_specs=out_specs, out_shape=out_shape, scratch_shapes=scratch,
        compiler_params=_cparams(("arbitrary", "arbitrary"), 32),
        name="gla_latent" if latent else "gla_ctx")(*args)


def _router_kernel(h_ref, w_ref, b_ref, idx_ref, gate_ref, rank_ref, cnt_ref, run_scr):
    step = pl.program_id(0)

    @pl.when(step == 0)
    def _():
        run_scr[...] = jnp.zeros(run_scr.shape, F32)

    logits = _dot(h_ref[...], w_ref[...].astype(BF16)) + b_ref[...]
    lane = lax.broadcasted_iota(jnp.int32, logits.shape, 1)
    idx_out = jnp.zeros(logits.shape, jnp.int32)
    val_out = jnp.zeros(logits.shape, F32)
    member = jnp.zeros(logits.shape, F32)
    sels = []
    m0 = None
    denom = None
    for kk in range(TOP_K):
        m = logits.max(axis=-1, keepdims=True)
        sel = jnp.min(jnp.where(logits == m, lane, LANES), axis=-1, keepdims=True)
        sels.append(sel)
        if kk == 0:
            m0 = m
        e = jnp.exp(m - m0)
        denom = e if denom is None else denom + e
        idx_out = jnp.where(lane == kk, sel, idx_out)
        val_out = jnp.where(lane == kk, e, val_out)
        member = jnp.where(lane == sel, 1.0, member)
        logits = jnp.where(lane == sel, -jnp.inf, logits)
    idx_ref[...] = idx_out
    gate_ref[...] = val_out / denom

    ti = lax.broadcasted_iota(jnp.int32, (ROUTER_TM, ROUTER_TM), 0)
    tj = lax.broadcasted_iota(jnp.int32, (ROUTER_TM, ROUTER_TM), 1)
    earlier = jnp.where(tj < ti, 1.0, 0.0).astype(BF16)
    before = _dot(earlier, member.astype(BF16)) + run_scr[...]
    rank_out = jnp.zeros(logits.shape, F32)
    for kk in range(TOP_K):
        r = jnp.sum(jnp.where(lane == sels[kk], before, 0.0), axis=-1, keepdims=True)
        rank_out = jnp.where(lane == kk, r, rank_out)
    rank_ref[...] = rank_out.astype(jnp.int32)
    run_scr[...] = run_scr[...] + jnp.sum(member, axis=0, keepdims=True)
    cnt_ref[...] = run_scr[...]


def _router(h2, w_pad, b_pad):
    tm = ROUTER_TM
    tok_spec = pl.BlockSpec((tm, LANES), lambda i: (i, 0))
    return pl.pallas_call(
        _router_kernel,
        grid=(N_TOK // tm,),
        in_specs=[pl.BlockSpec((tm, D_MODEL), lambda i: (i, 0)),
                  pl.BlockSpec((D_MODEL, LANES), lambda i: (0, 0)),
                  pl.BlockSpec((1, LANES), lambda i: (0, 0))],
        out_specs=[tok_spec, tok_spec, tok_spec, pl.BlockSpec((1, LANES), lambda i: (0, 0))],
        out_shape=[jax.ShapeDtypeStruct((N_TOK, LANES), jnp.int32), jax.ShapeDtypeStruct((N_TOK, LANES), F32),
                   jax.ShapeDtypeStruct((N_TOK, LANES), jnp.int32), jax.ShapeDtypeStruct((1, LANES), F32)],
        scratch_shapes=[pltpu.VMEM((1, LANES), F32)],
        compiler_params=_cparams(("arbitrary",), 32),
        name="router")(h2, w_pad, b_pad)


def _tile_idx(i, nu_ref):
    return jnp.minimum(i, nu_ref[0] - 1)


def _run_start(te_ref, i):
    prev = te_ref[jnp.maximum(i - 1, 0)]
    return (i == 0) | (te_ref[i] != prev)


def _stream_weights(te_ref, nx_ref, copy_for, n_copies, on_ready):
    j = pl.program_id(0)
    i = pl.program_id(1)
    e = te_ref[i]

    @pl.when((i == 0) & (j == 0))
    def _():
        for c in range(n_copies):
            copy_for(e, j, c).start()

    for c in range(n_copies):
        copy_for(e, j, c).wait()
    on_ready()
    ne = nx_ref[e]
    jn = j + (ne <= e).astype(jnp.int32)

    @pl.when(jn < pl.num_programs(0))
    def _():
        for c in range(n_copies):
            copy_for(ne, jn, c).start()


def _moe_gu_kernel(te_ref, nu_ref, nx_ref, x_ref, w_hbm, bg_ref, bu_ref, h_ref, stage, wg_s, wu_s, sems, *, layer):
    i = pl.program_id(1)

    def copy_for(e, jj, which):
        col = pl.multiple_of(which * EXPERT_FF + jj * MOE_TF, MOE_TF)
        return pltpu.make_async_copy(w_hbm.at[layer, e, :, pl.ds(col, MOE_TF)], stage.at[which], sems.at[which])

    def cast():
        wg_s[...] = stage[0].astype(BF16)
        wu_s[...] = stage[1].astype(BF16)

    @pl.when(i < nu_ref[0])
    def _():
        @pl.when(_run_start(te_ref, i))
        def _():
            _stream_weights(te_ref, nx_ref, copy_for, 2, cast)

        x = x_ref[...]
        gate = _dot(x, wg_s[...]) + bg_ref[...]
        up = _dot(x, wu_s[...]) + bu_ref[...]
        gate = jnp.minimum(gate, SWIGLU_LIMIT)
        up = jnp.clip(up, -SWIGLU_LIMIT, SWIGLU_LIMIT)
        glu = gate * _sigmoid(SWIGLU_ALPHA * gate)
        h_ref[...] = ((up + 1.0) * glu).astype(BF16)

    @pl.when(i >= nu_ref[0])
    def _():
        h_ref[...] = jnp.zeros(h_ref.shape, BF16)


def _moe_gu(xs, w_gu, b_gu4, layer, tile_expert, n_used, next_expert):
    nj = EXPERT_FF // MOE_TF
    grid_spec = pltpu.PrefetchScalarGridSpec(
        num_scalar_prefetch=3,
        grid=(nj, MOE_TILES),
        in_specs=[
            pl.BlockSpec((MOE_TM, D_MODEL), lambda j, i, te, nu, nx: (_tile_idx(i, nu), 0)),
            pl.BlockSpec(memory_space=pl.ANY),
            pl.BlockSpec((None, None, 1, MOE_TF), lambda j, i, te, nu, nx: (layer, te[_tile_idx(i, nu)], 0, j)),
            pl.BlockSpec((None, None, 1, MOE_TF), lambda j, i, te, nu, nx: (layer, te[_tile_idx(i, nu)], 0, nj + j)),
        ],
        out_specs=pl.BlockSpec((MOE_TM, MOE_TF), lambda j, i, te, nu, nx: (i, j)),
        scratch_shapes=[pltpu.VMEM((2, D_MODEL, MOE_TF), F32), pltpu.VMEM((D_MODEL, MOE_TF), BF16),
                        pltpu.VMEM((D_MODEL, MOE_TF), BF16), pltpu.SemaphoreType.DMA((2,))])
    return pl.pallas_call(
        functools.partial(_moe_gu_kernel, layer=layer), grid_spec=grid_spec,
        out_shape=jax.ShapeDtypeStruct((MOE_SLOTS, EXPERT_FF), BF16),
        compiler_params=_cparams(("arbitrary", "arbitrary"), 44),
        name="moe_gate_up")(tile_expert, n_used, next_expert, xs, w_gu, b_gu4, b_gu4)


def _moe_down_kernel(te_ref, nu_ref, nx_ref, h_ref, w_hbm, b_ref, y_ref, stage, w_s, sems, *, layer):
    i = pl.program_id(1)

    def copy_for(e, jj, which):
        col = pl.multiple_of(jj * MOE_TN, MOE_TN)
        return pltpu.make_async_copy(w_hbm.at[layer, e, :, pl.ds(col, MOE_TN)], stage, sems.at[0])

    def cast():
        w_s[...] = stage[...].astype(BF16)

    @pl.when(i < nu_ref[0])
    def _():
        @pl.when(_run_start(te_ref, i))
        def _():
            _stream_weights(te_ref, nx_ref, copy_for, 1, cast)

        y = _dot(h_ref[...], w_s[...]) + b_ref[...]
        for c in range(Y_TILES):
            words = _pack_words(y[:, c * LANES:(c + 1) * LANES], y[:, Y_HALF + c * LANES:Y_HALF + (c + 1) * LANES])
            y_ref[pl.ds(c, MOE_TM, stride=Y_TILES), :] = words

    @pl.when(i >= nu_ref[0])
    def _():
        y_ref[...] = jnp.zeros(y_ref.shape, U32)


def _moe_down(hs, w_down, b_down4, layer, tile_expert, n_used, next_expert):
    grid_spec = pltpu.PrefetchScalarGridSpec(
        num_scalar_prefetch=3,
        grid=(Y_PARTS, MOE_TILES),
        in_specs=[
            pl.BlockSpec((MOE_TM, EXPERT_FF), lambda j, i, te, nu, nx: (_tile_idx(i, nu), 0)),
            pl.BlockSpec(memory_space=pl.ANY),
            pl.BlockSpec((None, None, 1, MOE_TN), lambda j, i, te, nu, nx: (layer, te[_tile_idx(i, nu)], 0, j)),
        ],
        out_specs=pl.BlockSpec((None, MOE_TM * Y_TILES, LANES), lambda j, i, te, nu, nx: (j, i, 0)),
        scratch_shapes=[pltpu.VMEM((EXPERT_FF, MOE_TN), F32), pltpu.VMEM((EXPERT_FF, MOE_TN), BF16),
                        pltpu.SemaphoreType.DMA((1,))])
    ys = pl.pallas_call(
        functools.partial(_moe_down_kernel, layer=layer), grid_spec=grid_spec,
        out_shape=jax.ShapeDtypeStruct((Y_PARTS, MOE_SLOTS * Y_TILES, LANES), U32),
        compiler_params=_cparams(("arbitrary", "arbitrary"), 44),
        name="moe_down")(tile_expert, n_used, next_expert, hs, w_down, b_down4)
    return ys.reshape(Y_PARTS, MOE_SLOTS, Y_TILES, LANES)


def _moe_plan(top_idx, rank, counts):
    experts = jnp.arange(N_EXPERTS, dtype=jnp.int32)
    padded = (counts + MOE_TM - 1) // MOE_TM * MOE_TM
    pad_end = jnp.cumsum(padded)
    pad_start = pad_end - padded
    onehot = top_idx[:, :, None] == experts[None, None, :]
    dest = jnp.sum(jnp.where(onehot, pad_start[None, None, :], 0), axis=-1).astype(jnp.int32) + rank
    tok = jnp.repeat(jnp.arange(N_TOK, dtype=jnp.int32), TOP_K)
    slot_tok = jnp.zeros((MOE_SLOTS,), jnp.int32).at[dest.reshape(-1)].set(tok, unique_indices=True)
    n_used = (pad_end[-1] // MOE_TM).astype(jnp.int32).reshape(1)
    tile_start = jnp.arange(MOE_TILES, dtype=jnp.int32) * MOE_TM
    tile_expert = jnp.minimum(jnp.sum(pad_end[None, :] <= tile_start[:, None], axis=1), N_EXPERTS - 1)
    has = counts > 0
    later = has[None, :] & (experts[None, :] > experts[:, None])
    nxt_later = jnp.min(jnp.where(later, experts[None, :], N_EXPERTS), axis=1)
    first = jnp.min(jnp.where(has, experts, N_EXPERTS))
    next_expert = jnp.where(nxt_later < N_EXPERTS, nxt_later, first).astype(jnp.int32)
    return slot_tok, dest, tile_expert.astype(jnp.int32), n_used, next_expert


def _moe(h2, h2_rows, lw, weights, layer):
    idx_pad, gate_pad, rank_pad, cnt = _router(h2, lw['router_w'], lw['router_b'])
    top_idx = idx_pad[:, :TOP_K]
    rank = rank_pad[:, :TOP_K]
    counts = cnt[0, :N_EXPERTS].astype(jnp.int32)
    slot_tok, dest, tile_expert, n_used, next_expert = _moe_plan(top_idx, rank, counts)
    xs = _gather_x(h2_rows, slot_tok)
    hs = _moe_gu(xs, weights['w_gu'], weights['b_gu'], layer, tile_expert, n_used, next_expert)
    ys = _moe_down(hs, weights['w_down'], weights['b_down'], layer, tile_expert, n_used, next_expert)
    return ys, dest, gate_pad


def _layer(x, cond16, lw, weights, layer, caches, tabs):
    state_gla, state_ret, cache_gqa_k, cache_gqa_v, cache_nat_k, cache_nat_v = caches
    mod = _adaln_mod(cond16, weights['mod_w'], weights['mod_b'], layer)
    mod3 = mod.reshape(16 * 6, 1, D_MODEL)

    h = _ln_mod(x, mod3)
    proj = _in_proj(h, lw['w_in'])

    ya_c, st_a = _gla(proj, lw['gla_up'], lw['gla_ub'], lw['gla_norm_g'], latent=False)
    ya_l = _gla(proj, lw['gla_up'], lw['gla_ub'], lw['gla_norm_g'], latent=True, state=state_gla, layer=layer)[0]
    yb_c, st_b = _ret(proj, tabs['ret_lg'], lw['ret_gn_g'], lw['ret_gn_b'], latent=False)
    yb_l = _ret(proj, tabs['ret_lg'], lw['ret_gn_g'], lw['ret_gn_b'], latent=True,
                cos=tabs['cos64'], sin=tabs['sin64'], state=state_ret, layer=layer)[0]
    yc_c, kc = _ctx_attn(proj, COL_C_Q, COL_C_K, COL_C_V, GQA_KV_HEADS, GQA_HEADS // GQA_KV_HEADS,
                         lw['gqa_qn_g'], lw['gqa_kn_g'], norm=True, name="gqa_ctx")
    yc_l = _gqa_lat(proj, cache_gqa_k, cache_gqa_v, layer, lw['gqa_qn_g'], lw['gqa_kn_g'],
                    tabs['cos128'], tabs['sin128'])
    yd_c = _ctx_attn(proj, COL_D_Q, COL_D_K, COL_D_V, NAT_HEADS, 1, lw['gqa_qn_g'], lw['gqa_kn_g'],
                     norm=False, name="nat_ctx")[0]
    yd_l = _nat_lat(proj, cache_nat_k, cache_nat_v, layer, lw['nat_bias'])

    mix = _out_proj([ya_c, yb_c, yc_c, yd_c], [ya_l, yb_l, yc_l, yd_l], weights['w_o'], layer)
    x1, h2, h2_rows = _res_ln1(x, mix, mod3, lw['ln1_g'], lw['ln1_b'])
    y_rows, dest, gate_pad = _moe(h2, h2_rows, lw, weights, layer)
    x2 = _res_ln2(x1, y_rows, dest, gate_pad, mod3, lw['ln2_g'], lw['ln2_b'])

    vc = proj[:N_PROMPT, COL_C_V:COL_C_V + GQA_KV_HEADS * GQA_DIM]
    kd = proj[:N_PROMPT, COL_D_K:COL_D_K + NAT_HEADS * NAT_DIM]
    vd = proj[:N_PROMPT, COL_D_V:COL_D_V + NAT_HEADS * NAT_DIM]
    new = (st_a, st_b,
           kc.reshape(BATCH, SEQ, GQA_KV_HEADS, GQA_DIM), vc.reshape(BATCH, SEQ, GQA_KV_HEADS, GQA_DIM),
           kd.reshape(BATCH, SEQ, NAT_HEADS, NAT_DIM), vd.reshape(BATCH, SEQ, NAT_HEADS, NAT_DIM))
    return x2, new


def _prep_layer_weights(l, w_in, gla_lr_up, gla_lr_b, gla_norm_g, ret_gn_g, ret_gn_b, gqa_qn_g, gqa_kn_g, nat_rpb,
                        ln1_g, ln1_b, router_w, router_b, ln2_g, ln2_b):
    wi = w_in[l]
    w_in_p = jnp.concatenate([wi[:, :ALR_SRC], wi[:, ALR_SRC + ALR_W:], wi[:, ALR_SRC:ALR_SRC + ALR_W],
                              jnp.zeros((D_MODEL, PROJ_PAD - wi.shape[1]), wi.dtype)], axis=1).astype(BF16)
    up = jnp.zeros((2, LANES, GLA_HEADS * GLA_DK), F32)
    for d in range(2):
        up = up.at[d, d * GLA_RANK:(d + 1) * GLA_RANK, :].set(gla_lr_up[l, d])
    rw = jnp.zeros((D_MODEL, LANES), F32).at[:, :N_EXPERTS].set(router_w[l])
    rb = jnp.full((1, LANES), NEG_BIG, F32).at[0, :N_EXPERTS].set(router_b[l])
    return {
        'w_in': w_in_p,
        'gla_up': up, 'gla_ub': gla_lr_b[l].reshape(2, 1, -1), 'gla_norm_g': gla_norm_g[l].reshape(1, -1),
        'ret_gn_g': ret_gn_g[l].reshape(1, -1), 'ret_gn_b': ret_gn_b[l].reshape(1, -1),
        'gqa_qn_g': gqa_qn_g[l].reshape(1, -1), 'gqa_kn_g': gqa_kn_g[l].reshape(1, -1),
        'nat_bias': _nat_bias_table(nat_rpb[l]),
        'ln1_g': ln1_g[l].reshape(1, -1), 'ln1_b': ln1_b[l].reshape(1, -1),
        'router_w': rw, 'router_b': rb,
        'ln2_g': ln2_g[l].reshape(1, -1), 'ln2_b': ln2_b[l].reshape(1, -1),
    }


def kernel(x_prompt, x_sample, state_gla, state_ret, cache_gqa_k, cache_gqa_v, cache_nat_k, cache_nat_v, c, c_ctx, mod_w, mod_b, w_in, gla_lr_up, gla_lr_b, gla_norm_g, ret_gn_g, ret_gn_b, gqa_qn_g, gqa_kn_g, nat_rpb, w_o, ln1_g, ln1_b, router_w, router_b, w_gu, b_gu, w_down, b_down, ln2_g, ln2_b):
    x = jnp.concatenate([x_prompt.reshape(N_PROMPT, D_MODEL), x_sample.reshape(N_SAMPLE, D_MODEL)], axis=0)
    cond16 = jnp.zeros((16, D_MODEL), F32).at[0].set(c_ctx).at[1:1 + DEC_BATCH].set(c)
    cos128, sin128 = _rope_tables(DEC_SEQ, GQA_DIM // 4)
    cos64, sin64 = _rope_tables(DEC_SEQ, RET_DK // 4)
    tabs = {'cos128': cos128, 'sin128': sin128, 'cos64': cos64, 'sin64': sin64,
            'ret_lg': jnp.asarray(_ret_log_decays(), F32)}
    weights = {'mod_w': mod_w, 'mod_b': mod_b.reshape(DEPTH, 1, -1), 'w_o': w_o,
               'w_gu': w_gu, 'b_gu': b_gu.reshape(DEPTH, N_EXPERTS, 1, -1),
               'w_down': w_down, 'b_down': b_down.reshape(DEPTH, N_EXPERTS, 1, -1)}
    caches = (state_gla.reshape(DEC_BATCH, DEPTH, 2, GLA_HEADS // 2, 2 * GLA_DK, GLA_DV),
              state_ret.reshape(DEC_BATCH, DEPTH, 2, RET_HEADS // 2, 2 * RET_DK, RET_DV),
              cache_gqa_k.reshape(DEC_BATCH, DEPTH, PAST_LEN, GQA_KV_HEADS * GQA_DIM),
              cache_gqa_v.reshape(DEC_BATCH, DEPTH, PAST_LEN, GQA_KV_HEADS * GQA_DIM),
              cache_nat_k.reshape(DEC_BATCH, DEPTH, PAST_LEN, NAT_HEADS * NAT_DIM),
              cache_nat_v.reshape(DEC_BATCH, DEPTH, PAST_LEN, NAT_HEADS * NAT_DIM))
    news = []
    for l in range(DEPTH):
        lw = _prep_layer_weights(l, w_in, gla_lr_up, gla_lr_b, gla_norm_g, ret_gn_g, ret_gn_b, gqa_qn_g, gqa_kn_g,
                                 nat_rpb, ln1_g, ln1_b, router_w, router_b, ln2_g, ln2_b)
        x, new = _layer(x, cond16, lw, weights, l, caches, tabs)
        news.append(new)
    y_p = x[:N_PROMPT].reshape(BATCH, SEQ, D_MODEL)
    y_s = x[N_PROMPT:].reshape(DEC_BATCH, DEC_SEQ, D_MODEL)
    stacked = tuple(jnp.stack([news[l][k] for l in range(DEPTH)], axis=1) for k in range(6))
    return (y_p, y_s) + stacked
```
